```python
import math
import jax
import jax.numpy as jnp
from jax import lax
import numpy as np

D_MODEL = 4096
BATCH = 1
SEQ = 16384
DEPTH = 4

W_GROUP = D_MODEL // 4
D_MIX = 4 * W_GROUP

HY_W = W_GROUP
HY_ORDER = 2
HY_BANDS = 16
HY_EMB = 2 * HY_BANDS + 1
HY_FILTER_WIDTH = 64
HY_INNER = 2
HY_SIN_W = 1.0
HY_TARGET = 1e-2
HY_SHORT_PCT = 0.3
HY_LONG_PCT = 1.5

RET_H = 4
RET_DV = W_GROUP // RET_H
RET_DK = RET_DV // 2
RET_CHUNK = 128

ATT_DH = 128
ATT_H = W_GROUP // ATT_DH
DIL_PAIRS = ((128, 1), (512, 4), (2048, 16))

ML_H = 4
ML_DV = W_GROUP // ML_H
ML_DK = ML_DV // 2
ML_CHUNK = 64
ML_GATES = 2 * 2 * ML_H

FFN_HIDDEN = ((-((-8 * D_MODEL) // 3) + 255) // 256) * 256

IN_SPLITS = (3 * HY_W,
             RET_H * RET_DK, RET_H * RET_DK, RET_H * RET_DV, RET_H * RET_DV,
             ATT_H * ATT_DH, ATT_H * ATT_DH, ATT_H * ATT_DH,
             ML_H * ML_DK, ML_H * ML_DK, ML_H * ML_DV, ML_H * ML_DV, ML_GATES)
D_IN = sum(IN_SPLITS)
NEG = -1e30
EPS = 1e-6

kernel_name = 'hybrid_bidir_parallel_heads_encoder'


def _split_points():
    pts, acc = [], 0
    for s in IN_SPLITS[:-1]:
        acc += s
        pts.append(acc)
    return pts


def rms_norm(x):
    xf = x.astype(jnp.float32)
    return (xf * lax.rsqrt(jnp.mean(xf * xf, -1, keepdims=True) + EPS)).astype(x.dtype)


def head_rms(x, gain=None):
    xf = x.astype(jnp.float32)
    y = xf * lax.rsqrt(jnp.mean(xf * xf, -1, keepdims=True) + EPS)
    return y if gain is None else y * gain.astype(jnp.float32)


def centred_conv3(u, w):
    up = jnp.pad(u, ((0, 0), (1, 1), (0, 0)))
    return up[:, :-2] * w[0] + up[:, 1:-1] * w[1] + up[:, 2:] * w[2]


def hyena_filters(L, w1, b1, w2, b2, w3):
    f32 = jnp.float32
    pos = jnp.arange(L, dtype=f32)
    t = jnp.linspace(0.0, 1.0, L, dtype=f32)
    ang = (2.0 * math.pi / L) * pos
    freqs = jnp.linspace(1e-4, HY_BANDS - 1, HY_BANDS, dtype=f32)
    z = jnp.concatenate([t[:, None], jnp.cos(ang[:, None] * freqs), -jnp.sin(ang[:, None] * freqs)], -1)
    h = jnp.sin(HY_SIN_W * (z @ w1.astype(f32) + b1.astype(f32)))
    for i in range(HY_INNER):
        h = jnp.sin(HY_SIN_W * (h @ w2[i].astype(f32) + b2[i].astype(f32)))
    h = (h @ w3.astype(f32)).reshape(L, HY_ORDER, 2, HY_W)
    max_decay = math.log(HY_TARGET) / HY_SHORT_PCT
    min_decay = math.log(HY_TARGET) / HY_LONG_PCT
    deltas = jnp.abs(jnp.linspace(min_decay, max_decay, HY_W, dtype=f32))
    h = h * jnp.exp(-t[:, None] * deltas)[:, None, None, :]
    fwd, bwd = h[:, :, 0], h[:, :, 1]
    full = jnp.concatenate([fwd, jnp.zeros((1, HY_ORDER, HY_W), f32), jnp.flip(bwd[1:], 0)], 0)
    full = full / jnp.sum(jnp.abs(full), 0, keepdims=True)
    return jnp.fft.rfft(full, axis=0)


def hyena_mixer(u, short_w, w1, b1, w2, b2, w3, bias):
    B, L, _ = u.shape
    u = centred_conv3(u, short_w)
    v, x1, x2 = jnp.split(u.astype(jnp.float32), 3, -1)
    kf = hyena_filters(L, w1, b1, w2, b2, w3)
    z = v
    for o, gate in enumerate((x1, x2)):
        zc = jnp.fft.irfft(jnp.fft.rfft(z, n=2 * L, axis=1) * kf[:, o], n=2 * L, axis=1)[:, :L]
        z = gate * (zc + bias[o].astype(jnp.float32) * z)
    return z


def retention_dir(q, k, v, log_gamma):
    B, H, L, dk = q.shape
    dv = v.shape[-1]
    C = RET_CHUNK
    N = L // C
    q = q.reshape(B, H, N, C, dk)
    k = k.reshape(B, H, N, C, dk)
    v = v.reshape(B, H, N, C, dv)
    pos = jnp.arange(C, dtype=jnp.float32)
    lg = log_gamma[:, None]
    diff = pos[:, None] - pos[None, :]
    intra_decay = jnp.exp(jnp.where(diff >= 0, lg[:, :, None] * diff, -jnp.inf))
    scores = jnp.einsum('bhnid,bhnjd->bhnij', q, k) * intra_decay[None, :, None]
    y = jnp.einsum('bhnij,bhnje->bhnie', scores, v)
    k_w = jnp.exp(lg * (C - 1 - pos))
    q_w = jnp.exp(lg * (pos + 1))
    kv = jnp.einsum('bhncd,bhnce->nbhde', k * k_w[None, :, None, :, None], v)
    chunk_decay = jnp.exp(log_gamma * C)[None, :, None, None]

    def step(state, kv_n):
        return state * chunk_decay + kv_n, state

    _, s_prev = lax.scan(step, jnp.zeros((B, H, dk, dv), jnp.float32), kv)
    y = y + jnp.einsum('bhncd,nbhde->bhnce', q * q_w[None, :, None, :, None], s_prev)
    return y.reshape(B, H, L, dv)


def retention_mixer(q, k, v, g, decay_logit):
    B, L, _ = q.shape
    heads = lambda a, d: a.reshape(B, L, RET_H, d).transpose(0, 2, 1, 3).astype(jnp.float32)
    q = heads(q, RET_DK)
    k = heads(k, RET_DK) * (RET_DK ** -0.5)
    v = heads(v, RET_DV)
    lg = jax.nn.log_sigmoid(decay_logit.astype(jnp.float32))
    flip = lambda a: jnp.flip(a, axis=2)
    y = retention_dir(q, k, v, lg[0]) + flip(retention_dir(flip(q), flip(k), flip(v), lg[1]))
    y = head_rms(y)
    y = y.transpose(0, 2, 1, 3).reshape(B, L, RET_H * RET_DV)
    return jax.nn.silu(g.astype(jnp.float32)) * y


def dilated_branch(q, k, v, window, dilation, slopes):
    B, L, H, Dh = q.shape
    steps = window // (2 * dilation)
    blk = steps
    n = L // dilation
    nb = -(-n // blk)
    pad = nb * blk - n
    strided = lambda a: a.reshape(B, n, dilation, H, Dh)
    qb = jnp.pad(strided(q), ((0, 0), (0, pad), (0, 0), (0, 0), (0, 0))).reshape(B, nb, blk, dilation, H, Dh)

    def band(a):
        ap = jnp.pad(strided(a), ((0, 0), (blk, pad + blk), (0, 0), (0, 0), (0, 0)))
        ap = ap.reshape(B, nb + 2, blk, dilation, H, Dh)
        return jnp.concatenate([ap[:, :-2], ap[:, 1:-1], ap[:, 2:]], axis=2)

    kb, vb = band(k), band(v)
    s = jnp.einsum('bnqrhd,bnkrhd->bnrhqk', qb, kb)
    qi = jnp.arange(blk)
    ki = jnp.arange(3 * blk)
    off = ki[None, :] - blk - qi[:, None]
    kpos = jnp.arange(nb)[:, None, None] * blk + ki[None, None, :] - blk
    valid = (jnp.abs(off) <= steps)[None] & (kpos >= 0) & (kpos < n)
    dist = (jnp.abs(off) * dilation).astype(jnp.float32)
    s = s - slopes[:, None, None] * dist
    s = jnp.where(valid[None, :, None, None], s, NEG)
    m = jnp.max(s, -1, keepdims=True)
    p = jnp.exp(s - m)
    den = jnp.sum(p, -1, keepdims=True)
    o = jnp.einsum('bnrhqk,bnkrhd->bnqrhd', p / den, vb)
    lse = jnp.transpose((m + jnp.log(den))[..., 0], (0, 1, 4, 2, 3))
    o = o.reshape(B, nb * blk, dilation, H, Dh)[:, :n].reshape(B, L, H, Dh)
    lse = lse.reshape(B, nb * blk, dilation, H)[:, :n].reshape(B, L, H)
    return o, lse


def dilated_attention(q, k, v, qk_gain):
    B, L, _ = q.shape
    slopes = 2.0 ** (-8.0 * jnp.arange(1, ATT_H + 1, dtype=jnp.float32) / ATT_H)
    q = head_rms(q.reshape(B, L, ATT_H, ATT_DH), qk_gain[0]) * (ATT_DH ** -0.5)
    k = head_rms(k.reshape(B, L, ATT_H, ATT_DH), qk_gain[1])
    v = v.reshape(B, L, ATT_H, ATT_DH).astype(jnp.float32)
    outs, lses = [], []
    for window, dilation in DIL_PAIRS:
        o, lse = dilated_branch(q, k, v, window, dilation, slopes)
        outs.append(o)
        lses.append(lse)
    w = jax.nn.softmax(jnp.stack(lses, 0), axis=0)
    o = jnp.sum(w[..., None] * jnp.stack(outs, 0), 0)
    return o.reshape(B, L, ATT_H * ATT_DH)


def mlstm_dir(q, k, v, ig, lf):
    B, H, L, dk = q.shape
    dv = v.shape[-1]
    C = ML_CHUNK
    N = L // C
    q = q.reshape(B, H, N, C, dk)
    k = k.reshape(B, H, N, C, dk)
    v = v.reshape(B, H, N, C, dv)
    ig = ig.reshape(B, H, N, C)
    b = jnp.cumsum(lf.reshape(B, H, N, C), -1)
    b_last = b[..., -1]
    a = b_last[..., None] - b + ig
    m_loc = jnp.max(a, -1)
    wa = jnp.exp(a - m_loc[..., None])
    kv = jnp.einsum('bhncd,bhnce->nbhde', k * wa[..., None], v)
    ksum = jnp.transpose(jnp.sum(k * wa[..., None], 3), (2, 0, 1, 3))

    def step(carry, xs):
        c_s, n_s, m_s = carry
        kv_n, k_n, mloc_n, blast_n = xs
        m_new = jnp.maximum(blast_n + m_s, mloc_n)
        sp = jnp.exp(blast_n + m_s - m_new)
        sc = jnp.exp(mloc_n - m_new)
        c_new = sp[..., None, None] * c_s + sc[..., None, None] * kv_n
        n_new = sp[..., None] * n_s + sc[..., None] * k_n
        return (c_new, n_new, m_new), (c_s, n_s, m_s)

    init = (jnp.zeros((B, H, dk, dv), jnp.float32), jnp.zeros((B, H, dk), jnp.float32),
            jnp.zeros((B, H), jnp.float32))
    _, (c_prev, n_prev, m_prev) = lax.scan(
        step, init, (kv, ksum, jnp.transpose(m_loc, (2, 0, 1)), jnp.transpose(b_last, (2, 0, 1))))
    inter_log = b + jnp.transpose(m_prev, (1, 2, 0))[..., None]
    idx = jnp.arange(C)
    lower = idx[:, None] >= idx[None, :]
    dlog = jnp.where(lower, b[..., :, None] - b[..., None, :] + ig[..., None, :], -jnp.inf)
    m_t = jnp.maximum(inter_log, jnp.max(dlog, -1))
    s = jnp.einsum('bhnid,bhnjd->bhnij', q, k) * jnp.exp(dlog - m_t[..., None])
    wi = jnp.exp(inter_log - m_t)
    num = jnp.einsum('bhnij,bhnje->bhnie', s, v) + wi[..., None] * jnp.einsum('bhncd,nbhde->bhnce', q, c_prev)
    den = jnp.sum(s, -1) + wi * jnp.einsum('bhncd,nbhd->bhnc', q, n_prev)
    h = num / jnp.maximum(jnp.abs(den), jnp.exp(-m_t))[..., None]
    return h.reshape(B, H, L, dv)


def mlstm_mixer(q, k, v, o, gates, gate_bias, norm_gain):
    B, L, _ = q.shape
    heads = lambda a, d: a.reshape(B, L, ML_H, d).transpose(0, 2, 1, 3).astype(jnp.float32)
    q = heads(q, ML_DK)
    k = heads(k, ML_DK) * (ML_DK ** -0.5)
    v = heads(v, ML_DV)
    g = (gates.astype(jnp.float32) + gate_bias.astype(jnp.float32)).reshape(B, L, 2, 2, ML_H)
    g = jnp.transpose(g, (2, 3, 0, 4, 1))
    ig = g[:, 0]
    lf = jax.nn.log_sigmoid(g[:, 1])
    flip = lambda a: jnp.flip(a, axis=2)
    h = mlstm_dir(q, k, v, ig[0], lf[0]) + flip(mlstm_dir(flip(q), flip(k), flip(v), flip(ig[1]), flip(lf[1])))
    h = head_rms(h, norm_gain.reshape(ML_H, 1, ML_DV))
    h = h.transpose(0, 2, 1, 3).reshape(B, L, ML_H * ML_DV)
    return jax.nn.sigmoid(o.astype(jnp.float32)) * h


def setup_inputs(seed: int = 0) -> dict:
    key = jax.random.key(seed)
    ks = jax.random.split(key, 21)
    f32 = jnp.float32
    nrm = lambda kk, shape, s: jax.random.normal(kk, shape, f32) * s
    x = nrm(ks[0], (BATCH, SEQ, D_MODEL), 1.0)
    c = nrm(ks[1], (BATCH, D_MODEL), 1.0)
    ada_w = nrm(ks[2], (D_MODEL, 6 * D_MODEL), 0.2 * D_MODEL ** -0.5)
    ada_b = nrm(ks[3], (6 * D_MODEL,), 0.02)
    ada_table = nrm(ks[4], (DEPTH, 6, D_MODEL), 0.1)
    w_in = nrm(ks[5], (DEPTH, D_MODEL, D_IN), D_MODEL ** -0.5)
    w_out = nrm(ks[6], (DEPTH, D_MIX, D_MODEL), D_MIX ** -0.5)
    hy_short = nrm(ks[7], (DEPTH, 3, 3 * HY_W), 3 ** -0.5)
    hy_w1 = nrm(ks[8], (DEPTH, HY_EMB, HY_FILTER_WIDTH), HY_EMB ** -0.5)
    hy_b1 = nrm(ks[9], (DEPTH, HY_FILTER_WIDTH), 0.1)
    hy_w2 = nrm(ks[10], (DEPTH, HY_INNER, HY_FILTER_WIDTH, HY_FILTER_WIDTH), HY_FILTER_WIDTH ** -0.5)
    hy_b2 = nrm(ks[11], (DEPTH, HY_INNER, HY_FILTER_WIDTH), 0.1)
    hy_w3 = nrm(ks[12], (DEPTH, HY_FILTER_WIDTH, HY_ORDER * 2 * HY_W), HY_FILTER_WIDTH ** -0.5)
    hy_bias = nrm(ks[13], (DEPTH, HY_ORDER, HY_W), 1.0)
    ret_base = jnp.log(2.0 ** (5.0 + jnp.arange(RET_H, dtype=f32)) - 1.0)
    ret_decay = ret_base + nrm(ks[14], (DEPTH, 2, RET_H), 0.05)
    att_qk_gain = 1.0 + nrm(ks[15], (DEPTH, 2, ATT_DH), 0.02)
    gate_base = jnp.stack([jnp.zeros((ML_H,), f32), jnp.linspace(3.0, 6.0, ML_H, dtype=f32)])
    ml_gate_bias = (gate_base[None, None] + nrm(ks[16], (DEPTH, 2, 2, ML_H), 0.1)).reshape(DEPTH, ML_GATES)
    ml_norm_gain = 1.0 + nrm(ks[17], (DEPTH, ML_H * ML_DV), 0.02)
    ffn_w1 = nrm(ks[18], (DEPTH, D_MODEL, FFN_HIDDEN), D_MODEL ** -0.5)
    ffn_w3 = nrm(ks[19], (DEPTH, D_MODEL, FFN_HIDDEN), D_MODEL ** -0.5)
    ffn_w2 = nrm(ks[20], (DEPTH, FFN_HIDDEN, D_MODEL), FFN_HIDDEN ** -0.5)
    return {'x': x, 'c': c, 'ada_w': ada_w, 'ada_b': ada_b, 'ada_table': ada_table,
            'w_in': w_in, 'w_out': w_out, 'hy_short': hy_short, 'hy_w1': hy_w1, 'hy_b1': hy_b1,
            'hy_w2': hy_w2, 'hy_b2': hy_b2, 'hy_w3': hy_w3, 'hy_bias': hy_bias,
            'ret_decay': ret_decay, 'att_qk_gain': att_qk_gain, 'ml_gate_bias': ml_gate_bias,
            'ml_norm_gain': ml_norm_gain, 'ffn_w1': ffn_w1, 'ffn_w3': ffn_w3, 'ffn_w2': ffn_w2}


def reference(x, c, ada_w, ada_b, ada_table, w_in, w_out, hy_short, hy_w1, hy_b1, hy_w2, hy_b2,
              hy_w3, hy_bias, ret_decay, att_qk_gain, ml_gate_bias, ml_norm_gain,
              ffn_w1, ffn_w3, ffn_w2):
    split_pts = _split_points()
    mod_shared = jax.nn.silu(c) @ ada_w + ada_b
    for l in range(DEPTH):
        mod = (mod_shared + ada_table[l].reshape(-1))[:, None, :]
        sh1, sc1, g1, sh2, sc2, g2 = jnp.split(mod, 6, -1)
        h = rms_norm(x) * (1 + sc1) + sh1
        proj = h @ w_in[l]
        (hy_in, r_q, r_k, r_v, r_g, a_q, a_k, a_v,
         m_q, m_k, m_v, m_o, m_g) = jnp.split(proj, split_pts, -1)
        y_a = hyena_mixer(hy_in, hy_short[l], hy_w1[l], hy_b1[l], hy_w2[l], hy_b2[l], hy_w3[l], hy_bias[l])
        y_b = retention_mixer(r_q, r_k, r_v, r_g, ret_decay[l])
        y_c = dilated_attention(a_q, a_k, a_v, att_qk_gain[l])
        y_d = mlstm_mixer(m_q, m_k, m_v, m_o, m_g, ml_gate_bias[l], ml_norm_gain[l])
        y = jnp.concatenate([y_a, y_b, y_c, y_d], -1).astype(x.dtype)
        x = x + g1 * (y @ w_out[l])
        h = rms_norm(x) * (1 + sc2) + sh2
        x = x + g2 * ((jax.nn.silu(h @ ffn_w1[l]) * (h @ ffn_w3[l])) @ ffn_w2[l])
    return x
```

```python
import functools
import math

import numpy as np
import jax
import jax.numpy as jnp
from jax import lax
from jax.experimental import pallas as pl
from jax.experimental.pallas import tpu as pltpu

F32 = jnp.float32
BF16 = jnp.bfloat16
HIGHEST = lax.Precision.HIGHEST

EPS = 1e-6
NEG = -1e30

V7X_VMEM_LIMIT_BYTES = 56 * 1024 * 1024
LANES = 128

W_GROUP = 1024
RET_H, RET_DK, RET_DV = 4, 128, 256
ATT_H, ATT_DH = 8, 128
ATT_HALF_STEPS = 64
ATT_DILATIONS = (1, 4, 16)
ML_H, ML_DK, ML_DV = 4, 128, 256
HY_BANDS = 16
HY_W = W_GROUP
HY_FILTER_WIDTH = 64
HY_TARGET, HY_SHORT_PCT, HY_LONG_PCT = 1e-2, 0.3, 1.5
HY_FEAT_PAD = 40
DFT_N2 = 128
CHUNK = 256


def _cparams(*sem):
    return pltpu.CompilerParams(dimension_semantics=sem, vmem_limit_bytes=V7X_VMEM_LIMIT_BYTES)


def _dot(a, b):
    return jnp.dot(a, b, preferred_element_type=F32)


def _dot_nt(a, b):
    return lax.dot_general(a, b, (((1,), (1,)), ((), ())), preferred_element_type=F32)


def _dot_tn(a, b):
    return lax.dot_general(a, b, (((0,), (0,)), ((), ())), preferred_element_type=F32)


def _ada_kernel(c_ref, w_ref, b_ref, o_ref):
    c = c_ref[...]
    s = c * jax.nn.sigmoid(c)
    o_ref[...] = jnp.dot(s, w_ref[...], preferred_element_type=F32, precision=HIGHEST) + b_ref[...]


def ada_modulation(c, ada_w, ada_b):
    d, n = ada_w.shape
    tn = 512
    c8 = jnp.broadcast_to(c.reshape(1, d), (8, d))
    out = pl.pallas_call(
        _ada_kernel,
        grid=(n // tn,),
        in_specs=[pl.BlockSpec((8, d), lambda j: (0, 0)),
                  pl.BlockSpec((d, tn), lambda j: (0, j)),
                  pl.BlockSpec((1, tn), lambda j: (0, j))],
        out_specs=pl.BlockSpec((8, tn), lambda j: (0, j)),
        out_shape=jax.ShapeDtypeStruct((8, n), F32),
        compiler_params=_cparams("parallel"),
        name="ada_modulation",
    )(c8, ada_w, ada_b.reshape(1, n))
    return out[0:1]


NORM_ROWS = 64


def _normalise_into(x_ref, sc_ref, sh_ref, h_ref):
    tm = x_ref.shape[0]
    sc = sc_ref[...]
    sh = sh_ref[...]

    def body(r, carry):
        rows = pl.ds(pl.multiple_of(r * NORM_ROWS, NORM_ROWS), NORM_ROWS)
        x = x_ref[rows, :]
        ms = jnp.mean(x * x, axis=-1, keepdims=True)
        h_ref[rows, :] = (x * lax.rsqrt(ms + EPS) * sc + sh).astype(h_ref.dtype)
        return carry

    lax.fori_loop(0, tm // NORM_ROWS, body, 0)


def _norm_proj_kernel(x_ref, sc_ref, sh_ref, w_ref, wg_ref, o_ref, g_ref, h_ref):
    @pl.when(pl.program_id(1) == 0)
    def _():
        _normalise_into(x_ref, sc_ref, sh_ref, h_ref)
        g_ref[...] = _dot(h_ref[...], wg_ref[...])

    o_ref[...] = _dot(h_ref[...], w_ref[...]).astype(o_ref.dtype)


def norm_proj(x, scale1p, shift, w, wg, tm=512, tn=1024):
    m, d = x.shape
    n = w.shape[1]
    ng = wg.shape[1]
    return pl.pallas_call(
        _norm_proj_kernel,
        grid=(m // tm, n // tn),
        in_specs=[pl.BlockSpec((tm, d), lambda i, j: (i, 0)),
                  pl.BlockSpec((1, d), lambda i, j: (0, 0)),
                  pl.BlockSpec((1, d), lambda i, j: (0, 0)),
                  pl.BlockSpec((d, tn), lambda i, j: (0, j)),
                  pl.BlockSpec((d, ng), lambda i, j: (0, 0))],
        out_specs=[pl.BlockSpec((tm, tn), lambda i, j: (i, j)),
                   pl.BlockSpec((tm, ng), lambda i, j: (i, 0))],
        out_shape=[jax.ShapeDtypeStruct((m, n), BF16), jax.ShapeDtypeStruct((m, ng), F32)],
        scratch_shapes=[pltpu.VMEM((tm, d), BF16)],
        compiler_params=_cparams("parallel", "arbitrary"),
        name="norm_proj",
    )(x, scale1p, shift, w, wg)


def _norm_swiglu_kernel(x_ref, sc_ref, sh_ref, w1_ref, w3_ref, o_ref, h_ref):
    @pl.when(pl.program_id(1) == 0)
    def _():
        _normalise_into(x_ref, sc_ref, sh_ref, h_ref)

    h = h_ref[...]
    a = _dot(h, w1_ref[...])
    b = _dot(h, w3_ref[...])
    o_ref[...] = (a * jax.nn.sigmoid(a) * b).astype(o_ref.dtype)


def norm_swiglu(x, scale1p, shift, w1, w3, tm=512, tn=512):
    m, d = x.shape
    n = w1.shape[1]
    return pl.pallas_call(
        _norm_swiglu_kernel,
        grid=(m // tm, n // tn),
        in_specs=[pl.BlockSpec((tm, d), lambda i, j: (i, 0)),
                  pl.BlockSpec((1, d), lambda i, j: (0, 0)),
                  pl.BlockSpec((1, d), lambda i, j: (0, 0)),
                  pl.BlockSpec((d, tn), lambda i, j: (0, j)),
                  pl.BlockSpec((d, tn), lambda i, j: (0, j))],
        out_specs=pl.BlockSpec((tm, tn), lambda i, j: (i, j)),
        out_shape=jax.ShapeDtypeStruct((m, n), BF16),
        scratch_shapes=[pltpu.VMEM((tm, d), BF16)],
        compiler_params=_cparams("parallel", "arbitrary"),
        name="norm_swiglu",
    )(x, scale1p, shift, w1, w3)


def _mm_residual_kernel(y_ref, w_ref, x_ref, g_ref, o_ref, acc_ref, *, nk):
    k = pl.program_id(2)
    part = _dot(y_ref[...], w_ref[...])

    @pl.when(k == 0)
    def _():
        acc_ref[...] = part

    @pl.when(k > 0)
    def _():
        acc_ref[...] += part

    @pl.when(k == nk - 1)
    def _():
        o_ref[...] = x_ref[...] + g_ref[...] * acc_ref[...]


def _mm_residual_1k_kernel(y_ref, w_ref, x_ref, g_ref, o_ref):
    o_ref[...] = x_ref[...] + g_ref[...] * _dot(y_ref[...], w_ref[...])


def mm_residual(y, w, x, gate, tm, tn, tk):
    m, kk = y.shape
    n = w.shape[1]
    nk = kk // tk
    if nk == 1:
        return pl.pallas_call(
            _mm_residual_1k_kernel,
            grid=(m // tm, n // tn),
            in_specs=[pl.BlockSpec((tm, kk), lambda i, j: (i, 0)),
                      pl.BlockSpec((kk, tn), lambda i, j: (0, j)),
                      pl.BlockSpec((tm, tn), lambda i, j: (i, j)),
                      pl.BlockSpec((1, tn), lambda i, j: (0, j))],
            out_specs=pl.BlockSpec((tm, tn), lambda i, j: (i, j)),
            out_shape=jax.ShapeDtypeStruct((m, n), F32),
            compiler_params=_cparams("parallel", "arbitrary"),
            name="mm_residual",
        )(y, w, x, gate)
    return pl.pallas_call(
        functools.partial(_mm_residual_kernel, nk=nk),
        grid=(m // tm, n // tn, nk),
        in_specs=[pl.BlockSpec((tm, tk), lambda i, j, k: (i, k)),
                  pl.BlockSpec((tk, tn), lambda i, j, k: (k, j)),
                  pl.BlockSpec((tm, tn), lambda i, j, k: (i, j)),
                  pl.BlockSpec((1, tn), lambda i, j, k: (0, j))],
        out_specs=pl.BlockSpec((tm, tn), lambda i, j, k: (i, j)),
        out_shape=jax.ShapeDtypeStruct((m, n), F32),
        scratch_shapes=[pltpu.VMEM((tm, tn), F32)],
        compiler_params=_cparams("parallel", "arbitrary", "arbitrary"),
        name="mm_residual_ksplit",
    )(y, w, x, gate)


def _dft_tables(seq_len):
    n = 2 * seq_len
    n2 = DFT_N2
    n1 = n // n2
    half = n1 // 2
    kb = half + 1
    kbp = -(-kb // 8) * 8
    k1 = jnp.arange(kbp, dtype=jnp.int32)
    live = (k1 <= half)
    col = jnp.arange(n1, dtype=jnp.int32)
    ph = (2.0 * math.pi / n1) * ((k1[:, None] * col[None, :]) % n1).astype(F32)
    wf = jnp.stack([jnp.cos(ph), -jnp.sin(ph)], axis=1) * live[:, None, None]
    wf = wf.reshape(2 * kbp, n1)
    a = jnp.arange(n2, dtype=jnp.int32)
    mm = (a[None, None, :] * (k1[:, None, None] + n1 * a[None, :, None])) % n
    th = (2.0 * math.pi / n) * mm.astype(F32)
    c, s = jnp.cos(th), jnp.sin(th)
    tf = jnp.concatenate([jnp.concatenate([c, s], 2), jnp.concatenate([-s, c], 2)], 1)
    tf = tf * live[:, None, None]
    ti = jnp.transpose(tf, (0, 2, 1))
    wt = jnp.where((k1 == 0) | (k1 == half), 1.0, 2.0) * live / n
    row = jnp.arange(half, dtype=jnp.int32)
    ph2 = (2.0 * math.pi / n1) * ((row[:, None] * k1[None, :]) % n1).astype(F32)
    wi = jnp.stack([wt * jnp.cos(ph2), -wt * jnp.sin(ph2)], axis=2).reshape(half, 2 * kbp)
    return dict(wf_data=wf[:, :half].astype(BF16), wf_filt=wf.astype(BF16), tf=tf.astype(BF16),
                ti=ti.astype(BF16), wi=wi.astype(BF16), kbp=kbp, n1=n1)


def _outer_dft_kernel(w_ref, x_ref, o_ref):
    o_ref[...] = _dot(w_ref[...], x_ref[...]).astype(o_ref.dtype)


def outer_dft(w, x, tn=8192):
    r, k = w.shape
    ncol = x.shape[1]
    tn = min(tn, ncol)
    return pl.pallas_call(
        _outer_dft_kernel,
        grid=(ncol // tn,),
        in_specs=[pl.BlockSpec((r, k), lambda j: (0, 0)),
                  pl.BlockSpec((k, tn), lambda j: (0, j))],
        out_specs=pl.BlockSpec((r, tn), lambda j: (0, j)),
        out_shape=jax.ShapeDtypeStruct((r, ncol), BF16),
        compiler_params=_cparams("parallel"),
        name="hyena_outer_dft",
    )(w, x)


def _filter_spectrum_kernel(tf_ref, a_ref, inv_ref, o_ref):
    inv = inv_ref[...]
    for b in range(tf_ref.shape[0]):
        o_ref[b] = (_dot(tf_ref[b], a_ref[b]) * inv).astype(o_ref.dtype)


def filter_spectrum(tf, a, inv_l1, kb=4, tc=1024):
    kbp, r, ch = a.shape
    return pl.pallas_call(
        _filter_spectrum_kernel,
        grid=(kbp // kb, ch // tc),
        in_specs=[pl.BlockSpec((kb, r, r), lambda i, j: (i, 0, 0)),
                  pl.BlockSpec((kb, r, tc), lambda i, j: (i, 0, j)),
                  pl.BlockSpec((1, tc), lambda i, j: (0, j))],
        out_specs=pl.BlockSpec((kb, r, tc), lambda i, j: (i, 0, j)),
        out_shape=jax.ShapeDtypeStruct((kbp, r, ch), BF16),
        compiler_params=_cparams("parallel", "arbitrary"),
        name="hyena_filter_spectrum",
    )(tf, a, inv_l1)


def _spectral_conv_kernel(tf_ref, ti_ref, a_ref, k_ref, o_ref):
    half = a_ref.shape[1] // 2
    for b in range(tf_ref.shape[0]):
        z = _dot(tf_ref[b], a_ref[b])
        zr, zi = z[:half], z[half:]
        kr = k_ref[b, :half, :].astype(F32)
        ki = k_ref[b, half:, :].astype(F32)
        y = jnp.concatenate([zr * kr - zi * ki, zr * ki + zi * kr], axis=0).astype(BF16)
        o_ref[b] = _dot(ti_ref[b], y).astype(o_ref.dtype)


def spectral_conv(tf, ti, a, kf, order, kb=4):
    kbp, r, ch = a.shape
    return pl.pallas_call(
        _spectral_conv_kernel,
        grid=(kbp // kb,),
        in_specs=[pl.BlockSpec((kb, r, r), lambda i: (i, 0, 0)),
                  pl.BlockSpec((kb, r, r), lambda i: (i, 0, 0)),
                  pl.BlockSpec((kb, r, ch), lambda i: (i, 0, 0)),
                  pl.BlockSpec((kb, r, ch), lambda i: (i, 0, order))],
        out_specs=pl.BlockSpec((kb, r, ch), lambda i: (i, 0, 0)),
        out_shape=jax.ShapeDtypeStruct((kbp, r, ch), BF16),
        compiler_params=_cparams("parallel"),
        name="hyena_spectral_conv",
    )(tf, ti, a, kf)


def _outer_inverse_gate_kernel(w_ref, g_ref, gate_ref, zp_ref, bias_ref, o_ref):
    zc = _dot(w_ref[...], g_ref[...])
    zp = zp_ref[...].astype(F32)
    o_ref[...] = (gate_ref[...].astype(F32) * (zc + bias_ref[...] * zp)).astype(o_ref.dtype)


def outer_inverse_gate(wi, g, gate, z_prev, bias_row, tn=8192):
    r, k = wi.shape
    ncol = g.shape[1]
    tn = min(tn, ncol)
    return pl.pallas_call(
        _outer_inverse_gate_kernel,
        grid=(ncol // tn,),
        in_specs=[pl.BlockSpec((r, k), lambda j: (0, 0)),
                  pl.BlockSpec((k, tn), lambda j: (0, j)),
                  pl.BlockSpec((r, tn), lambda j: (0, j)),
                  pl.BlockSpec((r, tn), lambda j: (0, j)),
                  pl.BlockSpec((1, tn), lambda j: (0, 0))],
        out_specs=pl.BlockSpec((r, tn), lambda j: (0, j)),
        out_shape=jax.ShapeDtypeStruct((r, ncol), BF16),
        compiler_params=_cparams("parallel"),
        name="hyena_outer_inverse_gate",
    )(wi, g, gate, z_prev, bias_row)


def _short_conv_kernel(u_ref, prev_ref, next_ref, w_ref, v_ref, x1_ref, x2_ref):
    tr = u_ref.shape[0]
    ch = v_ref.shape[1]
    row = lax.broadcasted_iota(jnp.int32, (tr, ch), 0)
    for part, o_ref in enumerate((v_ref, x1_ref, x2_ref)):
        cols = slice(part * ch, (part + 1) * ch)
        x = u_ref[:, cols].astype(F32)
        before = jnp.where(row == 0, prev_ref[0, :, cols].astype(F32), pltpu.roll(x, 1, axis=0))
        after = jnp.where(row == tr - 1, next_ref[0, :, cols].astype(F32), pltpu.roll(x, tr - 1, axis=0))
        w = w_ref[:, cols]
        o_ref[...] = (before * w[0:1] + x * w[1:2] + after * w[2:3]).astype(o_ref.dtype)


def short_conv(proj, short_w, tr=512):
    seq = proj.shape[0]
    width = 3 * HY_W
    nb = seq // tr
    zero = jnp.zeros((1, width), proj.dtype)
    prev_rows = jnp.concatenate([zero, proj[tr - 1::tr, :width][:-1]], 0).reshape(nb, 1, width)
    next_rows = jnp.concatenate([proj[tr::tr, :width], zero], 0).reshape(nb, 1, width)
    out = jax.ShapeDtypeStruct((seq, HY_W), BF16)
    return pl.pallas_call(
        _short_conv_kernel,
        grid=(nb,),
        in_specs=[pl.BlockSpec((tr, width), lambda i: (i, 0)),
                  pl.BlockSpec((1, 1, width), lambda i: (i, 0, 0)),
                  pl.BlockSpec((1, 1, width), lambda i: (i, 0, 0)),
                  pl.BlockSpec((3, width), lambda i: (0, 0))],
        out_specs=[pl.BlockSpec((tr, HY_W), lambda i: (i, 0))] * 3,
        out_shape=[out, out, out],
        compiler_params=_cparams("parallel"),
        name="hyena_short_conv",
    )(proj, prev_rows, next_rows, short_w)


def _filter_mlp_kernel(z_ref, w1_ref, b1_ref, w2_ref, b2_ref, o_ref):
    hdot = functools.partial(jnp.dot, preferred_element_type=F32, precision=HIGHEST)
    h = jnp.sin(hdot(z_ref[...], w1_ref[...]) + b1_ref[...])
    for i in range(w2_ref.shape[0]):
        h = jnp.sin(hdot(h, w2_ref[i]) + b2_ref[i])
    o_ref[...] = h


def filter_mlp(feat, w1p, b1, w2, b2, tr=1024):
    seq, fp = feat.shape
    wd = w1p.shape[1]
    ni = w2.shape[0]
    tr = min(tr, seq)
    return pl.pallas_call(
        _filter_mlp_kernel,
        grid=(seq // tr,),
        in_specs=[pl.BlockSpec((tr, fp), lambda i: (i, 0)),
                  pl.BlockSpec((fp, wd), lambda i: (0, 0)),
                  pl.BlockSpec((1, wd), lambda i: (0, 0)),
                  pl.BlockSpec((ni, wd, wd), lambda i: (0, 0, 0)),
                  pl.BlockSpec((ni, 1, wd), lambda i: (0, 0, 0))],
        out_specs=pl.BlockSpec((tr, wd), lambda i: (i, 0)),
        out_shape=jax.ShapeDtypeStruct((seq, wd), F32),
        compiler_params=_cparams("parallel"),
        name="hyena_filter_mlp",
    )(feat, w1p, b1.reshape(1, wd), w2, b2.reshape(ni, 1, wd))


def _filter_expand_kernel(h_ref, t_ref, w3_ref, dl_ref, o_ref, s_ref):
    @pl.when(pl.program_id(0) == 0)
    def _():
        s_ref[...] = jnp.zeros_like(s_ref)

    f = jnp.dot(h_ref[...], w3_ref[0], preferred_element_type=F32, precision=HIGHEST)
    f = f * jnp.exp(-t_ref[...] * dl_ref[...])
    o_ref[...] = f.astype(o_ref.dtype)
    tr, ch = f.shape
    s_ref[...] += jnp.sum(jnp.abs(f).reshape(tr // 8, 8, ch), axis=0)


def filter_expand(h_full, t_full, w3_halves, deltas2, tr=512):
    n, wd = h_full.shape
    ch = w3_halves.shape[2]
    tr = min(tr, n // 2)
    nb = n // tr
    return pl.pallas_call(
        _filter_expand_kernel,
        grid=(nb,),
        in_specs=[pl.BlockSpec((tr, wd), lambda i: (i, 0)),
                  pl.BlockSpec((tr, 1), lambda i: (i, 0)),
                  pl.BlockSpec((1, wd, ch), lambda i: (i // (nb // 2), 0, 0)),
                  pl.BlockSpec((1, ch), lambda i: (0, 0))],
        out_specs=[pl.BlockSpec((tr, ch), lambda i: (i, 0)),
                   pl.BlockSpec((8, ch), lambda i: (0, 0))],
        out_shape=[jax.ShapeDtypeStruct((n, ch), BF16), jax.ShapeDtypeStruct((8, ch), F32)],
        compiler_params=_cparams("arbitrary"),
        name="hyena_filter_expand",
    )(h_full, t_full, w3_halves, deltas2)


def _filter_positions(seq_len):
    pos = np.arange(seq_len, dtype=np.float64)
    t = np.linspace(0.0, 1.0, seq_len)
    ang = (2.0 * math.pi / seq_len) * pos
    freqs = np.linspace(1e-4, HY_BANDS - 1, HY_BANDS)
    feat = np.concatenate([t[:, None], np.cos(ang[:, None] * freqs), -np.sin(ang[:, None] * freqs)], -1)
    feat = np.pad(feat, ((0, 0), (0, HY_FEAT_PAD - feat.shape[1])))
    t_full = np.concatenate([t, np.zeros(1), t[:0:-1]])[:, None]
    max_decay = math.log(HY_TARGET) / HY_SHORT_PCT
    min_decay = math.log(HY_TARGET) / HY_LONG_PCT
    deltas = np.abs(np.linspace(min_decay, max_decay, HY_W))
    return (jnp.asarray(feat, F32), jnp.asarray(t_full, F32),
            jnp.asarray(np.tile(deltas, 2)[None, :], F32))


def hyena_filter_spectra(seq_len, tabs, consts, w1, b1, w2, b2, w3):
    feat, t_full, deltas2 = consts
    w1p = jnp.pad(w1, ((0, HY_FEAT_PAD - w1.shape[0]), (0, 0)))
    h = filter_mlp(feat, w1p, b1, w2, b2)
    h_full = jnp.concatenate([h, jnp.zeros((1, h.shape[1]), F32), jnp.flip(h[1:], 0)], 0)
    w3r = w3.reshape(w3.shape[0], 2, 2, HY_W)
    w3_halves = jnp.transpose(w3r, (2, 0, 1, 3)).reshape(2, w3.shape[0], 2 * HY_W)
    full, sabs = filter_expand(h_full, t_full, w3_halves, deltas2)
    inv_l1 = 1.0 / jnp.sum(sabs, axis=0, keepdims=True)
    n1 = tabs["n1"]
    a = outer_dft(tabs["wf_filt"], full.reshape(n1, DFT_N2 * 2 * HY_W))
    a = a.reshape(tabs["kbp"], 2 * DFT_N2, 2 * HY_W)
    return filter_spectrum(tabs["tf"], a, inv_l1)


def hyena_mixer(proj, tabs, kf, short_w, bias):
    seq = proj.shape[0]
    half = tabs["n1"] // 2
    ncol = DFT_N2 * HY_W
    v, x1, x2 = short_conv(proj, short_w)
    z = v
    for order, gate in enumerate((x1, x2)):
        a = outer_dft(tabs["wf_data"], z.reshape(half, ncol)).reshape(tabs["kbp"], 2 * DFT_N2, HY_W)
        g = spectral_conv(tabs["tf"], tabs["ti"], a, kf, order)
        bias_row = jnp.tile(bias[order].reshape(1, HY_W).astype(F32), (1, 8192 // HY_W))
        z = outer_inverse_gate(tabs["wi"], g.reshape(2 * tabs["kbp"], ncol), gate.reshape(half, ncol),
                               z.reshape(half, ncol), bias_row).reshape(seq, HY_W)
    return z


def _retention_kernel(lg_ref, q_ref, k_ref, v_ref, g_ref, o_ref, rstore_ref, s_ref, *, nc):
    h = pl.program_id(0)
    sweep = pl.program_id(1)
    i = pl.program_id(2)
    t = q_ref.shape[0]
    lgf = lg_ref[0, h]
    lgb = lg_ref[1, h]
    pos = lax.broadcasted_iota(jnp.int32, (t, 1), 0).astype(F32)
    chunk_len = jnp.full((1, RET_DV), float(t), F32)
    k = k_ref[...].astype(F32)
    v = v_ref[...]

    @pl.when(i == 0)
    def _():
        s_ref[...] = jnp.zeros_like(s_ref)

    @pl.when(sweep == 0)
    def _():
        n = nc - 1 - i
        rstore_ref[n] = s_ref[...]
        kw = (k * jnp.exp(lgb * pos)).astype(BF16)
        s_ref[...] = s_ref[...] * jnp.exp(lgb * chunk_len) + _dot_tn(kw, v)

    @pl.when(sweep == 1)
    def _():
        q = q_ref[...].astype(F32) * (RET_DK ** -0.5)
        ri = lax.broadcasted_iota(jnp.int32, (t, t), 0)
        ci = lax.broadcasted_iota(jnp.int32, (t, t), 1)
        diff = (ri - ci).astype(F32)
        decay = (jnp.where(diff >= 0, jnp.exp(lgf * jnp.maximum(diff, 0.0)), 0.0)
                 + jnp.where(diff <= 0, jnp.exp(lgb * jnp.maximum(-diff, 0.0)), 0.0))
        scores = _dot_nt(q.astype(BF16), k_ref[...]) * decay
        y = _dot(scores.astype(BF16), v)
        qf = (q * jnp.exp(lgf * (pos + 1.0))).astype(BF16)
        qb = (q * jnp.exp(lgb * (t - pos))).astype(BF16)
        y = y + _dot(qf, s_ref[...].astype(BF16)) + _dot(qb, rstore_ref[i].astype(BF16))
        kw = (k * jnp.exp(lgf * (t - 1.0 - pos))).astype(BF16)
        s_ref[...] = s_ref[...] * jnp.exp(lgf * chunk_len) + _dot_tn(kw, v)
        y = y * lax.rsqrt(jnp.mean(y * y, axis=-1, keepdims=True) + EPS)
        gt = g_ref[...].astype(F32)
        o_ref[...] = (gt * jax.nn.sigmoid(gt) * y).astype(o_ref.dtype)


def retention_mixer(proj, col0, log_decay, t=CHUNK):
    seq = proj.shape[0]
    nc = seq // t
    qb = col0 // RET_DK
    kb = qb + RET_H
    vb = (col0 + 2 * RET_H * RET_DK) // RET_DV
    gb = vb + RET_H

    def rows(sweep, i):
        return sweep * i + (1 - sweep) * (nc - 1 - i)

    return pl.pallas_call(
        functools.partial(_retention_kernel, nc=nc),
        grid=(RET_H, 2, nc),
        in_specs=[pl.BlockSpec(memory_space=pltpu.SMEM),
                  pl.BlockSpec((t, RET_DK), lambda h, s, i: (s * i, qb + h)),
                  pl.BlockSpec((t, RET_DK), lambda h, s, i: (rows(s, i), kb + h)),
                  pl.BlockSpec((t, RET_DV), lambda h, s, i: (rows(s, i), vb + h)),
                  pl.BlockSpec((t, RET_DV), lambda h, s, i: (s * i, gb + h))],
        out_specs=pl.BlockSpec((t, RET_DV), lambda h, s, i: (s * i, h)),
        out_shape=jax.ShapeDtypeStruct((seq, RET_H * RET_DV), BF16),
        scratch_shapes=[pltpu.VMEM((nc, RET_DK, RET_DV), F32), pltpu.VMEM((RET_DK, RET_DV), F32)],
        compiler_params=_cparams("arbitrary", "arbitrary", "arbitrary"),
        name="retention",
    )(log_decay, proj, proj, proj, proj)


def _log_sigmoid(x):
    return jnp.minimum(x, 0.0) - jnp.log(1.0 + jnp.exp(-jnp.abs(x)))


def _mlstm_gates(gc_ref, gr_ref, bias_ref, h, direction):
    bi = bias_ref[direction * 2 * ML_H + h]
    bf = bias_ref[direction * 2 * ML_H + ML_H + h]
    a = 2 * direction
    ig_c = gc_ref[0, :, a:a + 1] + bi
    lf_c = _log_sigmoid(gc_ref[0, :, a + 1:a + 2] + bf)
    ig_r = gr_ref[0, a:a + 1, :] + bi
    lf_r = _log_sigmoid(gr_ref[0, a + 1:a + 2, :] + bf)
    return ig_c, lf_c, ig_r, lf_r


def _cumsum_cols_rows(lf_c, lf_r, backward):
    t = lf_c.shape[0]
    ri = lax.broadcasted_iota(jnp.int32, (t, t), 0)
    ci = lax.broadcasted_iota(jnp.int32, (t, t), 1)
    tri = ((ri <= ci) if backward else (ri >= ci)).astype(F32)
    cum_c = jnp.dot(tri, jnp.broadcast_to(lf_c, (t, LANES)), preferred_element_type=F32, precision=HIGHEST)[:, 0:1]
    cum_r = lax.dot_general(jnp.broadcast_to(lf_r, (8, t)), tri, (((1,), (1,)), ((), ())),
                            preferred_element_type=F32, precision=HIGHEST)[0:1, :]
    return cum_c, cum_r


def _mlstm_state_step(k, v, ig_c, cum_c, total, c_ref, n_ref, m_ref):
    a = total - cum_c + ig_c
    m_loc = jnp.max(a, axis=0, keepdims=True)
    kw = k * jnp.exp(a - m_loc)
    kv = _dot_tn(kw.astype(BF16), v)
    ksum = jnp.sum(kw, axis=0, keepdims=True)
    m_old = m_ref[0:1, 0:1]
    m_new = jnp.maximum(total + m_old, m_loc)
    sp = jnp.exp(total + m_old - m_new)
    sc = jnp.exp(m_loc - m_new)
    c_ref[...] = sp * c_ref[...] + sc * kv
    n_ref[...] = sp * n_ref[...] + sc * jnp.broadcast_to(ksum, n_ref.shape)
    m_ref[...] = jnp.broadcast_to(m_new, m_ref.shape)


def _mlstm_output(qk, q, v, ig_r, cum_c, cum_r, c_prev, n_prev, m_prev, backward):
    t = q.shape[0]
    ri = lax.broadcasted_iota(jnp.int32, (t, t), 0)
    ci = lax.broadcasted_iota(jnp.int32, (t, t), 1)
    keep = (ri <= ci) if backward else (ri >= ci)
    dlog = jnp.where(keep, cum_c - cum_r + ig_r, -jnp.inf)
    inter = cum_c + m_prev
    m_t = jnp.maximum(inter, jnp.max(dlog, axis=-1, keepdims=True))
    s = qk * jnp.exp(dlog - m_t)
    wi = jnp.exp(inter - m_t)
    num = _dot(s.astype(BF16), v) + wi * _dot(q.astype(BF16), c_prev.astype(BF16))
    den = jnp.sum(s, axis=-1, keepdims=True) + wi * jnp.sum(q * n_prev, axis=-1, keepdims=True)
    return num / jnp.maximum(jnp.abs(den), jnp.exp(-m_t))


def _mlstm_kernel(bias_ref, q_ref, k_ref, v_ref, o_ref, gc_ref, gr_ref, gain_ref, out_ref,
                  cstore_ref, nstore_ref, mstore_ref, c_ref, n_ref, m_ref, *, nc):
    h = pl.program_id(0)
    sweep = pl.program_id(1)
    i = pl.program_id(2)
    k = k_ref[...].astype(F32) * (ML_DK ** -0.5)
    v = v_ref[...]

    @pl.when(i == 0)
    def _():
        c_ref[...] = jnp.zeros_like(c_ref)
        n_ref[...] = jnp.zeros_like(n_ref)
        m_ref[...] = jnp.zeros_like(m_ref)

    @pl.when(sweep == 0)
    def _():
        n = nc - 1 - i
        cstore_ref[n] = c_ref[...]
        nstore_ref[n] = n_ref[...]
        mstore_ref[n] = m_ref[...]
        ig_c, lf_c, _, lf_r = _mlstm_gates(gc_ref, gr_ref, bias_ref, h, 1)
        cum_c, _ = _cumsum_cols_rows(lf_c, lf_r, True)
        total = jnp.sum(lf_c, axis=0, keepdims=True)
        _mlstm_state_step(k, v, ig_c, cum_c, total, c_ref, n_ref, m_ref)

    @pl.when(sweep == 1)
    def _():
        q = q_ref[...].astype(F32)
        qk = _dot_nt(q_ref[...], k.astype(BF16))
        ig_c, lf_c, ig_r, lf_r = _mlstm_gates(gc_ref, gr_ref, bias_ref, h, 0)
        cum_c, cum_r = _cumsum_cols_rows(lf_c, lf_r, False)
        hf = _mlstm_output(qk, q, v, ig_r, cum_c, cum_r, c_ref[...], n_ref[0:1, :], m_ref[0:1, 0:1], False)
        total = jnp.sum(lf_c, axis=0, keepdims=True)
        _mlstm_state_step(k, v, ig_c, cum_c, total, c_ref, n_ref, m_ref)
        _, lb_c, igb_r, lb_r = _mlstm_gates(gc_ref, gr_ref, bias_ref, h, 1)
        cumb_c, cumb_r = _cumsum_cols_rows(lb_c, lb_r, True)
        hb = _mlstm_output(qk, q, v, igb_r, cumb_c, cumb_r, cstore_ref[i], nstore_ref[i][0:1, :],
                           mstore_ref[i][0:1, 0:1], True)
        y = hf + hb
        y = y * lax.rsqrt(jnp.mean(y * y, axis=-1, keepdims=True) + EPS) * gain_ref[...]
        out_ref[...] = (jax.nn.sigmoid(o_ref[...].astype(F32)) * y).astype(out_ref.dtype)


def mlstm_mixer(proj, gates, col0, gate_bias, norm_gain, t=CHUNK):
    seq = proj.shape[0]
    nc = seq // t
    qb = col0 // ML_DK
    kb = qb + ML_H
    vb = (col0 + 2 * ML_H * ML_DK) // ML_DV
    ob = vb + ML_H
    g = gates[:, :4 * ML_H].reshape(seq, 2, 2, ML_H)
    g = jnp.transpose(g, (3, 0, 1, 2)).reshape(ML_H, seq, 4)
    g_cols = g
    g_rows = jnp.transpose(g, (0, 2, 1))

    def rows(sweep, i):
        return sweep * i + (1 - sweep) * (nc - 1 - i)

    return pl.pallas_call(
        functools.partial(_mlstm_kernel, nc=nc),
        grid=(ML_H, 2, nc),
        in_specs=[pl.BlockSpec(memory_space=pltpu.SMEM),
                  pl.BlockSpec((t, ML_DK), lambda h, s, i: (s * i, qb + h)),
                  pl.BlockSpec((t, ML_DK), lambda h, s, i: (rows(s, i), kb + h)),
                  pl.BlockSpec((t, ML_DV), lambda h, s, i: (rows(s, i), vb + h)),
                  pl.BlockSpec((t, ML_DV), lambda h, s, i: (s * i, ob + h)),
                  pl.BlockSpec((1, t, 4), lambda h, s, i: (h, rows(s, i), 0)),
                  pl.BlockSpec((1, 4, t), lambda h, s, i: (h, 0, rows(s, i))),
                  pl.BlockSpec((1, ML_DV), lambda h, s, i: (0, h))],
        out_specs=pl.BlockSpec((t, ML_DV), lambda h, s, i: (s * i, h)),
        out_shape=jax.ShapeDtypeStruct((seq, ML_H * ML_DV), BF16),
        scratch_shapes=[pltpu.VMEM((nc, ML_DK, ML_DV), F32), pltpu.VMEM((nc, 8, ML_DK), F32),
                        pltpu.VMEM((nc, 8, LANES), F32), pltpu.VMEM((ML_DK, ML_DV), F32),
                        pltpu.VMEM((8, ML_DK), F32), pltpu.VMEM((8, LANES), F32)],
        compiler_params=_cparams("arbitrary", "arbitrary", "arbitrary"),
        name="mlstm",
    )(gate_bias, proj, proj, proj, proj, g_cols, g_rows, norm_gain.reshape(1, ML_H * ML_DV))


def _head_rms(x, gain):
    return x * lax.rsqrt(jnp.mean(x * x, axis=-1, keepdims=True) + EPS) * gain


def _band_attention_kernel(gain_ref, q_ref, kp_ref, kc_ref, kn_ref, vp_ref, vc_ref, vn_ref, o_ref, lse_ref,
                           *, dilation, nblk):
    i = pl.program_id(1)
    tq = q_ref.shape[0]
    hs = ATT_HALF_STEPS
    tk = tq + 2 * hs
    ri = lax.broadcasted_iota(jnp.int32, (tq, tk), 0)
    ci = lax.broadcasted_iota(jnp.int32, (tq, tk), 1)
    off = ci - hs - ri
    first_col = jnp.where(i > 0, 0, hs)
    end_col = jnp.where(i < nblk - 1, tk, tq + hs)
    valid = (jnp.abs(off) <= hs) & (ci >= first_col) & (ci < end_col)
    dist = (jnp.abs(off) * dilation).astype(F32)
    lane = lax.broadcasted_iota(jnp.int32, (tq, LANES), 1)
    lse_all = jnp.zeros((tq, LANES), F32)
    for h in range(ATT_H):
        cols = slice(h * ATT_DH, (h + 1) * ATT_DH)
        slope = 2.0 ** (-8.0 * (h + 1) / ATT_H)
        q = _head_rms(q_ref[:, cols].astype(F32), gain_ref[0:1, :]) * (ATT_DH ** -0.5)
        kk = jnp.concatenate([kp_ref[:, cols], kc_ref[:, cols], kn_ref[:, cols]], axis=0).astype(F32)
        kk = _head_rms(kk, gain_ref[1:2, :])
        vv = jnp.concatenate([vp_ref[:, cols], vc_ref[:, cols], vn_ref[:, cols]], axis=0)
        s = _dot_nt(q.astype(BF16), kk.astype(BF16)) - slope * dist
        s = jnp.where(valid, s, NEG)
        m = jnp.max(s, axis=-1, keepdims=True)
        p = jnp.exp(s - m)
        den = jnp.sum(p, axis=-1, keepdims=True)
        o = _dot(p.astype(BF16), vv) / den
        o_ref[:, cols] = o.astype(o_ref.dtype)
        lse_all = jnp.where(lane == h, m + jnp.log(den), lse_all)
    lse_ref[...] = lse_all


def band_attention(proj, col0, qk_gain, dilation, tq=256):
    seq, width = proj.shape
    n = seq // dilation
    tq = min(tq, n)
    nblk = n // tq
    hs = ATT_HALF_STEPS
    ratio = tq // hs
    view = proj.reshape(n, dilation * width)
    wb = width // W_GROUP
    cb = col0 // W_GROUP
    last_halo = n // hs - 1

    def cur(part):
        return pl.BlockSpec((tq, W_GROUP), lambda r, i: (i, r * wb + cb + part))

    def prev(part):
        return pl.BlockSpec((hs, W_GROUP), lambda r, i: (jnp.maximum(i * ratio - 1, 0), r * wb + cb + part))

    def nxt(part):
        return pl.BlockSpec((hs, W_GROUP),
                            lambda r, i: (jnp.minimum((i + 1) * ratio, last_halo), r * wb + cb + part))

    o, lse = pl.pallas_call(
        functools.partial(_band_attention_kernel, dilation=dilation, nblk=nblk),
        grid=(dilation, nblk),
        in_specs=[pl.BlockSpec((2, ATT_DH), lambda r, i: (0, 0)),
                  cur(0), prev(1), cur(1), nxt(1), prev(2), cur(2), nxt(2)],
        out_specs=[pl.BlockSpec((tq, W_GROUP), lambda r, i: (i, r)),
                   pl.BlockSpec((tq, LANES), lambda r, i: (i, r))],
        out_shape=[jax.ShapeDtypeStruct((n, dilation * W_GROUP), BF16),
                   jax.ShapeDtypeStruct((n, dilation * LANES), F32)],
        compiler_params=_cparams("parallel", "arbitrary"),
        name=f"band_attention_d{dilation}",
    )(qk_gain, view, view, view, view, view, view, view)
    return o.reshape(seq, W_GROUP), lse.reshape(seq, LANES)


def _merge_branches_kernel(o1_ref, o2_ref, o3_ref, l1_ref, l2_ref, l3_ref, out_ref):
    l1, l2, l3 = l1_ref[...], l2_ref[...], l3_ref[...]
    m = jnp.maximum(jnp.maximum(l1, l2), l3)
    e1, e2, e3 = jnp.exp(l1 - m), jnp.exp(l2 - m), jnp.exp(l3 - m)
    inv = 1.0 / (e1 + e2 + e3)
    w1, w2, w3 = e1 * inv, e2 * inv, e3 * inv
    for h in range(ATT_H):
        cols = slice(h * ATT_DH, (h + 1) * ATT_DH)
        out_ref[:, cols] = (w1[:, h:h + 1] * o1_ref[:, cols].astype(F32)
                            + w2[:, h:h + 1] * o2_ref[:, cols].astype(F32)
                            + w3[:, h:h + 1] * o3_ref[:, cols].astype(F32)).astype(out_ref.dtype)


def merge_branches(outs, lses, tr=512):
    seq = outs[0].shape[0]
    ospec = pl.BlockSpec((tr, W_GROUP), lambda i: (i, 0))
    lspec = pl.BlockSpec((tr, LANES), lambda i: (i, 0))
    return pl.pallas_call(
        _merge_branches_kernel,
        grid=(seq // tr,),
        in_specs=[ospec, ospec, ospec, lspec, lspec, lspec],
        out_specs=ospec,
        out_shape=jax.ShapeDtypeStruct((seq, W_GROUP), BF16),
        compiler_params=_cparams("parallel"),
        name="attention_merge",
    )(*outs, *lses)


def dilated_attention(proj, col0, qk_gain):
    outs, lses = [], []
    for d in ATT_DILATIONS:
        o, lse = band_attention(proj, col0, qk_gain, d)
        outs.append(o)
        lses.append(lse)
    return merge_branches(outs, lses)


def kernel(x, c, ada_w, ada_b, ada_table, w_in, w_out, hy_short, hy_w1, hy_b1, hy_w2, hy_b2, hy_w3, hy_bias,
           ret_decay, att_qk_gain, ml_gate_bias, ml_norm_gain, ffn_w1, ffn_w3, ffn_w2):
    batch, seq, d_model = x.shape
    depth = w_in.shape[0]
    d_main = 12 * W_GROUP
    hidden = ffn_w1.shape[2]
    hidden_pad = -(-hidden // 1024) * 1024

    tabs = _dft_tables(seq)
    consts = _filter_positions(seq)
    mod_shared = ada_modulation(c, ada_w, ada_b)

    outs = []
    for b in range(batch):
        xb = x[b]
        for l in range(depth):
            mod = (mod_shared[b:b + 1] + ada_table[l].reshape(1, -1)).reshape(6, d_model)
            sh1, sc1, g1, sh2, sc2, g2 = (mod[i:i + 1] for i in range(6))
            w_main = w_in[l, :, :d_main].astype(BF16)
            w_gate = jnp.pad(w_in[l, :, d_main:], ((0, 0), (0, LANES - (w_in.shape[2] - d_main)))).astype(BF16)
            proj, gates = norm_proj(xb, 1.0 + sc1, sh1, w_main, w_gate)

            kf = hyena_filter_spectra(seq, tabs, consts, hy_w1[l], hy_b1[l], hy_w2[l], hy_b2[l], hy_w3[l])
            y_a = hyena_mixer(proj, tabs, kf, hy_short[l], hy_bias[l])
            y_b = retention_mixer(proj, 3 * W_GROUP, jax.nn.log_sigmoid(ret_decay[l].astype(F32)))
            y_c = dilated_attention(proj, 6 * W_GROUP, att_qk_gain[l])
            y_d = mlstm_mixer(proj, gates, 9 * W_GROUP, ml_gate_bias[l], ml_norm_gain[l])
            y = jnp.concatenate([y_a, y_b, y_c, y_d], axis=-1)
            xb = mm_residual(y, w_out[l].astype(BF16), xb, g1, tm=1024, tn=512, tk=y.shape[1])

            pad = ((0, 0), (0, hidden_pad - hidden))
            w1 = jnp.pad(ffn_w1[l], pad).astype(BF16)
            w3 = jnp.pad(ffn_w3[l], pad).astype(BF16)
            w2 = jnp.pad(ffn_w2[l], (pad[1], pad[0])).astype(BF16)
            u = norm_swiglu(xb, 1.0 + sc2, sh2, w1, w3)
            xb = mm_residual(u, w2, xb, g2, tm=1024, tn=1024, tk=hidden_pad // 4)
        outs.append(xb)
    return jnp.stack(outs, 0)
```

```python
import functools
import math

import numpy as np
import jax
import jax.numpy as jnp
from jax import lax
from jax.experimental import pallas as pl
from jax.experimental.pallas import tpu as pltpu

F32 = jnp.float32
BF16 = jnp.bfloat16
HIGHEST = lax.Precision.HIGHEST

EPS = 1e-6
NEG = -1e30

V7X_VMEM_LIMIT_BYTES = 56 * 1024 * 1024
LANES = 128

W_GROUP = 1024
RET_H, RET_DK, RET_DV = 4, 128, 256
ATT_H, ATT_DH = 8, 128
ATT_HALF_STEPS = 64
ATT_DILATIONS = (1, 4, 16)
ML_H, ML_DK, ML_DV = 4, 128, 256
HY_BANDS = 16
HY_W = W_GROUP
HY_FILTER_WIDTH = 64
HY_TARGET, HY_SHORT_PCT, HY_LONG_PCT = 1e-2, 0.3, 1.5
HY_FEAT_PAD = 40
DFT_N2 = 128
CHUNK = 256


def _cparams(*sem):
    return pltpu.CompilerParams(dimension_semantics=sem, vmem_limit_bytes=V7X_VMEM_LIMIT_BYTES)


def _dot(a, b):
    return jnp.dot(a, b, preferred_element_type=F32)


def _dot_nt(a, b):
    return lax.dot_general(a, b, (((1,), (1,)), ((), ())), preferred_element_type=F32)


def _dot_tn(a, b):
    return lax.dot_general(a, b, (((0,), (0,)), ((), ())), preferred_element_type=F32)


def _ada_kernel(c_ref, w_ref, b_ref, o_ref):
    c = c_ref[...]
    s = c * jax.nn.sigmoid(c)
    o_ref[...] = jnp.dot(s, w_ref[...], preferred_element_type=F32, precision=HIGHEST) + b_ref[...]


def ada_modulation(c, ada_w, ada_b):
    d, n = ada_w.shape
    tn = 512
    c8 = jnp.broadcast_to(c.reshape(1, d), (8, d))
    out = pl.pallas_call(
        _ada_kernel,
        grid=(n // tn,),
        in_specs=[pl.BlockSpec((8, d), lambda j: (0, 0)),
                  pl.BlockSpec((d, tn), lambda j: (0, j)),
                  pl.BlockSpec((1, tn), lambda j: (0, j))],
        out_specs=pl.BlockSpec((8, tn), lambda j: (0, j)),
        out_shape=jax.ShapeDtypeStruct((8, n), F32),
        compiler_params=_cparams("parallel"),
        name="ada_modulation",
    )(c8, ada_w, ada_b.reshape(1, n))
    return out[0:1]


NORM_ROWS = 64


def _normalise_into(x_ref, sc_ref, sh_ref, h_ref):
    tm = x_ref.shape[0]
    sc = sc_ref[...]
    sh = sh_ref[...]

    def body(r, carry):
        rows = pl.ds(pl.multiple_of(r * NORM_ROWS, NORM_ROWS), NORM_ROWS)
        x = x_ref[rows, :]
        ms = jnp.mean(x * x, axis=-1, keepdims=True)
        h_ref[rows, :] = (x * lax.rsqrt(ms + EPS) * sc + sh).astype(h_ref.dtype)
        return carry

    lax.fori_loop(0, tm // NORM_ROWS, body, 0)


def _norm_proj_kernel(x_ref, sc_ref, sh_ref, w_ref, wg_ref, o_ref, g_ref, h_ref):
    @pl.when(pl.program_id(1) == 0)
    def _():
        _normalise_into(x_ref, sc_ref, sh_ref, h_ref)
        g_ref[...] = _dot(h_ref[...], wg_ref[...])

    o_ref[...] = _dot(h_ref[...], w_ref[...]).astype(o_ref.dtype)


def norm_proj(x, scale1p, shift, w, wg, tm=512, tn=1024):
    m, d = x.shape
    n = w.shape[1]
    ng = wg.shape[1]
    return pl.pallas_call(
        _norm_proj_kernel,
        grid=(m // tm, n // tn),
        in_specs=[pl.BlockSpec((tm, d), lambda i, j: (i, 0)),
                  pl.BlockSpec((1, d), lambda i, j: (0, 0)),
                  pl.BlockSpec((1, d), lambda i, j: (0, 0)),
                  pl.BlockSpec((d, tn), lambda i, j: (0, j)),
                  pl.BlockSpec((d, ng), lambda i, j: (0, 0))],
        out_specs=[pl.BlockSpec((tm, tn), lambda i, j: (i, j)),
                   pl.BlockSpec((tm, ng), lambda i, j: (i, 0))],
        out_shape=[jax.ShapeDtypeStruct((m, n), BF16), jax.ShapeDtypeStruct((m, ng), F32)],
        scratch_shapes=[pltpu.VMEM((tm, d), BF16)],
        compiler_params=_cparams("parallel", "arbitrary"),
        name="norm_proj",
    )(x, scale1p, shift, w, wg)


def _norm_swiglu_kernel(x_ref, sc_ref, sh_ref, w1_ref, w3_ref, o_ref, h_ref):
    @pl.when(pl.program_id(1) == 0)
    def _():
        _normalise_into(x_ref, sc_ref, sh_ref, h_ref)

    h = h_ref[...]
    a = _dot(h, w1_ref[...])
    b = _dot(h, w3_ref[...])
    o_ref[...] = (a * jax.nn.sigmoid(a) * b).astype(o_ref.dtype)


def norm_swiglu(x, scale1p, shift, w1, w3, tm=512, tn=512):
    m, d = x.shape
    n = w1.shape[1]
    return pl.pallas_call(
        _norm_swiglu_kernel,
        grid=(m // tm, n // tn),
        in_specs=[pl.BlockSpec((tm, d), lambda i, j: (i, 0)),
                  pl.BlockSpec((1, d), lambda i, j: (0, 0)),
                  pl.BlockSpec((1, d), lambda i, j: (0, 0)),
                  pl.BlockSpec((d, tn), lambda i, j: (0, j)),
                  pl.BlockSpec((d, tn), lambda i, j: (0, j))],
        out_specs=pl.BlockSpec((tm, tn), lambda i, j: (i, j)),
        out_shape=jax.ShapeDtypeStruct((m, n), BF16),
        scratch_shapes=[pltpu.VMEM((tm, d), BF16)],
        compiler_params=_cparams("parallel", "arbitrary"),
        name="norm_swiglu",
    )(x, scale1p, shift, w1, w3)


def _mm_residual_kernel(y_ref, w_ref, x_ref, g_ref, o_ref, acc_ref, *, nk):
    k = pl.program_id(2)
    part = _dot(y_ref[...], w_ref[...])

    @pl.when(k == 0)
    def _():
        acc_ref[...] = part

    @pl.when(k > 0)
    def _():
        acc_ref[...] += part

    @pl.when(k == nk - 1)
    def _():
        o_ref[...] = x_ref[...] + g_ref[...] * acc_ref[...]


def _mm_residual_1k_kernel(y_ref, w_ref, x_ref, g_ref, o_ref):
    o_ref[...] = x_ref[...] + g_ref[...] * _dot(y_ref[...], w_ref[...])


def mm_residual(y, w, x, gate, tm, tn, tk):
    m, kk = y.shape
    n = w.shape[1]
    nk = kk // tk
    if nk == 1:
        return pl.pallas_call(
            _mm_residual_1k_kernel,
            grid=(m // tm, n // tn),
            in_specs=[pl.BlockSpec((tm, kk), lambda i, j: (i, 0)),
                      pl.BlockSpec((kk, tn), lambda i, j: (0, j)),
                      pl.BlockSpec((tm, tn), lambda i, j: (i, j)),
                      pl.BlockSpec((1, tn), lambda i, j: (0, j))],
            out_specs=pl.BlockSpec((tm, tn), lambda i, j: (i, j)),
            out_shape=jax.ShapeDtypeStruct((m, n), F32),
            compiler_params=_cparams("parallel", "arbitrary"),
            name="mm_residual",
        )(y, w, x, gate)
    return pl.pallas_call(
        functools.partial(_mm_residual_kernel, nk=nk),
        grid=(m // tm, n // tn, nk),
        in_specs=[pl.BlockSpec((tm, tk), lambda i, j, k: (i, k)),
                  pl.BlockSpec((tk, tn), lambda i, j, k: (k, j)),
                  pl.BlockSpec((tm, tn), lambda i, j, k: (i, j)),
                  pl.BlockSpec((1, tn), lambda i, j, k: (0, j))],
        out_specs=pl.BlockSpec((tm, tn), lambda i, j, k: (i, j)),
        out_shape=jax.ShapeDtypeStruct((m, n), F32),
        scratch_shapes=[pltpu.VMEM((tm, tn), F32)],
        compiler_params=_cparams("parallel", "arbitrary", "arbitrary"),
        name="mm_residual_ksplit",
    )(y, w, x, gate)


def _dft_tables(seq_len):
    n = 2 * seq_len
    n2 = DFT_N2
    n1 = n // n2
    half = n1 // 2
    kb = half + 1
    kbp = -(-kb // 8) * 8
    k1 = jnp.arange(kbp, dtype=jnp.int32)
    live = (k1 <= half)
    col = jnp.arange(n1, dtype=jnp.int32)
    ph = (2.0 * math.pi / n1) * ((k1[:, None] * col[None, :]) % n1).astype(F32)
    wf = jnp.stack([jnp.cos(ph), -jnp.sin(ph)], axis=0) * live[None, :, None]
    wf = wf.reshape(2 * kbp, n1)
    a = jnp.arange(n2, dtype=jnp.int32)
    mm = (a[None, None, :] * (k1[:, None, None] + n1 * a[None, :, None])) % n
    th = (2.0 * math.pi / n) * mm.astype(F32)
    c, s = jnp.cos(th), jnp.sin(th)
    tf = jnp.concatenate([jnp.concatenate([c, s], 2), jnp.concatenate([-s, c], 2)], 1)
    tf = tf * live[:, None, None]
    ti = jnp.transpose(tf, (0, 2, 1))
    wt = jnp.where((k1 == 0) | (k1 == half), 1.0, 2.0) * live / n
    row = jnp.arange(half, dtype=jnp.int32)
    ph2 = (2.0 * math.pi / n1) * ((row[:, None] * k1[None, :]) % n1).astype(F32)
    wi = jnp.stack([wt * jnp.cos(ph2), -wt * jnp.sin(ph2)], axis=1).reshape(half, 2 * kbp)
    return dict(wf_data=wf[:, :half].astype(BF16), wf_filt=wf.astype(BF16), tf=tf.astype(BF16),
                ti=ti.astype(BF16), wi=wi.astype(BF16), kbp=kbp, n1=n1)


INNER_STEP = 8


def _outer_dft_kernel(w_ref, x_ref, o_ref):
    kbp = o_ref.shape[1]
    w = w_ref[...]
    for j in range(x_ref.shape[1]):
        a = _dot(w, x_ref[:, j, :].astype(BF16))
        o_ref[0, :, j, :] = a[:kbp]
        o_ref[1, :, j, :] = a[kbp:]


def outer_dft(w, x, tc=1024):
    r2, k = w.shape
    _, inner, ch = x.shape
    return pl.pallas_call(
        _outer_dft_kernel,
        grid=(inner // INNER_STEP, ch // tc),
        in_specs=[pl.BlockSpec((r2, k), lambda i, j: (0, 0)),
                  pl.BlockSpec((k, INNER_STEP, tc), lambda i, j: (0, i, j))],
        out_specs=pl.BlockSpec((2, r2 // 2, INNER_STEP, tc), lambda i, j: (0, 0, i, j)),
        out_shape=jax.ShapeDtypeStruct((2, r2 // 2, inner, ch), F32),
        compiler_params=_cparams("parallel", "parallel"),
        name="hyena_outer_dft",
    )(w, x)


def _filter_spectrum_kernel(tf_ref, a_ref, inv_ref, o_ref):
    inv = inv_ref[...]
    for b in range(tf_ref.shape[0]):
        x = jnp.concatenate([a_ref[0, b], a_ref[1, b]], axis=0).astype(BF16)
        o_ref[b] = (_dot(tf_ref[b], x) * inv).astype(o_ref.dtype)


def filter_spectrum(tf, a, inv_l1, kb=4, tc=1024):
    _, kbp, inner, ch = a.shape
    r = 2 * inner
    return pl.pallas_call(
        _filter_spectrum_kernel,
        grid=(kbp // kb, ch // tc),
        in_specs=[pl.BlockSpec((kb, r, r), lambda i, j: (i, 0, 0)),
                  pl.BlockSpec((2, kb, inner, tc), lambda i, j: (0, i, 0, j)),
                  pl.BlockSpec((1, tc), lambda i, j: (0, j))],
        out_specs=pl.BlockSpec((kb, r, tc), lambda i, j: (i, 0, j)),
        out_shape=jax.ShapeDtypeStruct((kbp, r, ch), BF16),
        compiler_params=_cparams("parallel", "arbitrary"),
        name="hyena_filter_spectrum",
    )(tf, a, inv_l1)


def _spectral_conv_kernel(tf_ref, ti_ref, a_ref, k_ref, o_ref):
    half = a_ref.shape[2]
    for b in range(tf_ref.shape[0]):
        x = jnp.concatenate([a_ref[0, b], a_ref[1, b]], axis=0).astype(BF16)
        z = _dot(tf_ref[b], x)
        zr, zi = z[:half], z[half:]
        kr = k_ref[b, :half, :].astype(F32)
        ki = k_ref[b, half:, :].astype(F32)
        y = jnp.concatenate([zr * kr - zi * ki, zr * ki + zi * kr], axis=0).astype(BF16)
        g = _dot(ti_ref[b], y)
        o_ref[0, b] = g[:half]
        o_ref[1, b] = g[half:]


def spectral_conv(tf, ti, a, kf, order, kb=4):
    _, kbp, inner, ch = a.shape
    r = 2 * inner
    return pl.pallas_call(
        _spectral_conv_kernel,
        grid=(kbp // kb,),
        in_specs=[pl.BlockSpec((kb, r, r), lambda i: (i, 0, 0)),
                  pl.BlockSpec((kb, r, r), lambda i: (i, 0, 0)),
                  pl.BlockSpec((2, kb, inner, ch), lambda i: (0, i, 0, 0)),
                  pl.BlockSpec((kb, r, ch), lambda i: (i, 0, order))],
        out_specs=pl.BlockSpec((2, kb, inner, ch), lambda i: (0, i, 0, 0)),
        out_shape=jax.ShapeDtypeStruct((2, kbp, inner, ch), F32),
        compiler_params=_cparams("parallel"),
        name="hyena_spectral_conv",
    )(tf, ti, a, kf)


def _outer_inverse_gate_kernel(w_ref, g_ref, gate_ref, zp_ref, bias_ref, o_ref):
    w = w_ref[...]
    bias = bias_ref[...]
    for j in range(g_ref.shape[2]):
        g = jnp.concatenate([g_ref[0, :, j, :], g_ref[1, :, j, :]], axis=0).astype(BF16)
        zc = _dot(w, g)
        o_ref[:, j, :] = gate_ref[:, j, :] * (zc + bias * zp_ref[:, j, :])


def outer_inverse_gate(wi, g, gate, z_prev, bias, tc=512):
    r, k2 = wi.shape
    _, kbp, inner, ch = g.shape
    zspec = pl.BlockSpec((r, INNER_STEP, tc), lambda i, j: (0, i, j))
    return pl.pallas_call(
        _outer_inverse_gate_kernel,
        grid=(inner // INNER_STEP, ch // tc),
        in_specs=[pl.BlockSpec((r, k2), lambda i, j: (0, 0)),
                  pl.BlockSpec((2, kbp, INNER_STEP, tc), lambda i, j: (0, 0, i, j)),
                  zspec, zspec,
                  pl.BlockSpec((1, tc), lambda i, j: (0, j))],
        out_specs=zspec,
        out_shape=jax.ShapeDtypeStruct((r, inner, ch), F32),
        compiler_params=_cparams("parallel", "parallel"),
        name="hyena_outer_inverse_gate",
    )(wi, g, gate, z_prev, bias)


def _short_conv_kernel(u_ref, prev_ref, next_ref, w_ref, v_ref, x1_ref, x2_ref):
    tr = u_ref.shape[0]
    ch = v_ref.shape[1]
    row = lax.broadcasted_iota(jnp.int32, (tr, ch), 0)
    for part, o_ref in enumerate((v_ref, x1_ref, x2_ref)):
        cols = slice(part * ch, (part + 1) * ch)
        x = u_ref[:, cols].astype(F32)
        before = jnp.where(row == 0, prev_ref[0, :, cols].astype(F32), pltpu.roll(x, 1, axis=0))
        after = jnp.where(row == tr - 1, next_ref[0, :, cols].astype(F32), pltpu.roll(x, tr - 1, axis=0))
        w = w_ref[:, cols]
        o_ref[...] = (before * w[0:1] + x * w[1:2] + after * w[2:3]).astype(o_ref.dtype)


def short_conv(proj, short_w, tr=512):
    seq = proj.shape[0]
    width = 3 * HY_W
    nb = seq // tr
    zero = jnp.zeros((1, width), proj.dtype)
    prev_rows = jnp.concatenate([zero, proj[tr - 1::tr, :width][:-1]], 0).reshape(nb, 1, width)
    next_rows = jnp.concatenate([proj[tr::tr, :width], zero], 0).reshape(nb, 1, width)
    out = jax.ShapeDtypeStruct((seq, HY_W), F32)
    return pl.pallas_call(
        _short_conv_kernel,
        grid=(nb,),
        in_specs=[pl.BlockSpec((tr, width), lambda i: (i, 0)),
                  pl.BlockSpec((1, 1, width), lambda i: (i, 0, 0)),
                  pl.BlockSpec((1, 1, width), lambda i: (i, 0, 0)),
                  pl.BlockSpec((3, width), lambda i: (0, 0))],
        out_specs=[pl.BlockSpec((tr, HY_W), lambda i: (i, 0))] * 3,
        out_shape=[out, out, out],
        compiler_params=_cparams("parallel"),
        name="hyena_short_conv",
    )(proj, prev_rows, next_rows, short_w)


def _filter_mlp_kernel(z_ref, w1_ref, b1_ref, w2_ref, b2_ref, o_ref):
    hdot = functools.partial(jnp.dot, preferred_element_type=F32, precision=HIGHEST)
    h = jnp.sin(hdot(z_ref[...], w1_ref[...]) + b1_ref[...])
    for i in range(w2_ref.shape[0]):
        h = jnp.sin(hdot(h, w2_ref[i]) + b2_ref[i])
    o_ref[...] = h


def filter_mlp(feat, w1p, b1, w2, b2, tr=1024):
    seq, fp = feat.shape
    wd = w1p.shape[1]
    ni = w2.shape[0]
    tr = min(tr, seq)
    return pl.pallas_call(
        _filter_mlp_kernel,
        grid=(seq // tr,),
        in_specs=[pl.BlockSpec((tr, fp), lambda i: (i, 0)),
                  pl.BlockSpec((fp, wd), lambda i: (0, 0)),
                  pl.BlockSpec((1, wd), lambda i: (0, 0)),
                  pl.BlockSpec((ni, wd, wd), lambda i: (0, 0, 0)),
                  pl.BlockSpec((ni, 1, wd), lambda i: (0, 0, 0))],
        out_specs=pl.BlockSpec((tr, wd), lambda i: (i, 0)),
        out_shape=jax.ShapeDtypeStruct((seq, wd), F32),
        compiler_params=_cparams("parallel"),
        name="hyena_filter_mlp",
    )(feat, w1p, b1.reshape(1, wd), w2, b2.reshape(ni, 1, wd))


def _filter_expand_kernel(h_ref, t_ref, w3_ref, dl_ref, o_ref, s_ref):
    @pl.when(pl.program_id(0) == 0)
    def _():
        s_ref[...] = jnp.zeros_like(s_ref)

    f = jnp.dot(h_ref[...], w3_ref[0], preferred_element_type=F32, precision=HIGHEST)
    f = f * jnp.exp(-t_ref[...] * dl_ref[...])
    o_ref[...] = f.astype(o_ref.dtype)
    tr, ch = f.shape
    s_ref[...] += jnp.sum(jnp.abs(f).reshape(tr // 8, 8, ch), axis=0)


def filter_expand(h_full, t_full, w3_halves, deltas2, tr=512):
    n, wd = h_full.shape
    ch = w3_halves.shape[2]
    tr = min(tr, n // 2)
    nb = n // tr
    return pl.pallas_call(
        _filter_expand_kernel,
        grid=(nb,),
        in_specs=[pl.BlockSpec((tr, wd), lambda i: (i, 0)),
                  pl.BlockSpec((tr, 1), lambda i: (i, 0)),
                  pl.BlockSpec((1, wd, ch), lambda i: (i // (nb // 2), 0, 0)),
                  pl.BlockSpec((1, ch), lambda i: (0, 0))],
        out_specs=[pl.BlockSpec((tr, ch), lambda i: (i, 0)),
                   pl.BlockSpec((8, ch), lambda i: (0, 0))],
        out_shape=[jax.ShapeDtypeStruct((n, ch), F32), jax.ShapeDtypeStruct((8, ch), F32)],
        compiler_params=_cparams("arbitrary"),
        name="hyena_filter_expand",
    )(h_full, t_full, w3_halves, deltas2)


def _filter_positions(seq_len):
    pos = np.arange(seq_len, dtype=np.float64)
    t = np.linspace(0.0, 1.0, seq_len)
    ang = (2.0 * math.pi / seq_len) * pos
    freqs = np.linspace(1e-4, HY_BANDS - 1, HY_BANDS)
    feat = np.concatenate([t[:, None], np.cos(ang[:, None] * freqs), -np.sin(ang[:, None] * freqs)], -1)
    feat = np.pad(feat, ((0, 0), (0, HY_FEAT_PAD - feat.shape[1])))
    t_full = np.concatenate([t, np.zeros(1), t[:0:-1]])[:, None]
    max_decay = math.log(HY_TARGET) / HY_SHORT_PCT
    min_decay = math.log(HY_TARGET) / HY_LONG_PCT
    deltas = np.abs(np.linspace(min_decay, max_decay, HY_W))
    return (jnp.asarray(feat, F32), jnp.asarray(t_full, F32),
            jnp.asarray(np.tile(deltas, 2)[None, :], F32))


def hyena_filter_spectra(seq_len, tabs, consts, w1, b1, w2, b2, w3):
    feat, t_full, deltas2 = consts
    w1p = jnp.pad(w1, ((0, HY_FEAT_PAD - w1.shape[0]), (0, 0)))
    h = filter_mlp(feat, w1p, b1, w2, b2)
    h_full = jnp.concatenate([h, jnp.zeros((1, h.shape[1]), F32), jnp.flip(h[1:], 0)], 0)
    w3r = w3.reshape(w3.shape[0], 2, 2, HY_W)
    w3_halves = jnp.transpose(w3r, (2, 0, 1, 3)).reshape(2, w3.shape[0], 2 * HY_W)
    full, sabs = filter_expand(h_full, t_full, w3_halves, deltas2)
    inv_l1 = 1.0 / jnp.sum(sabs, axis=0, keepdims=True)
    a = outer_dft(tabs["wf_filt"], full.reshape(tabs["n1"], DFT_N2, 2 * HY_W))
    return filter_spectrum(tabs["tf"], a, inv_l1)


def hyena_mixer(proj, tabs, kf, short_w, bias):
    seq = proj.shape[0]
    half = tabs["n1"] // 2
    view = (half, DFT_N2, HY_W)
    v, x1, x2 = short_conv(proj, short_w)
    z = v
    for order, gate in enumerate((x1, x2)):
        a = outer_dft(tabs["wf_data"], z.reshape(view))
        g = spectral_conv(tabs["tf"], tabs["ti"], a, kf, order)
        z = outer_inverse_gate(tabs["wi"], g, gate.reshape(view), z.reshape(view),
                               bias[order].reshape(1, HY_W).astype(F32)).reshape(seq, HY_W)
    return z


def _retention_kernel(lg_ref, q_ref, k_ref, v_ref, g_ref, o_ref, rstore_ref, s_ref, *, nc):
    h = pl.program_id(0)
    sweep = pl.program_id(1)
    i = pl.program_id(2)
    t = q_ref.shape[0]
    lgf = lg_ref[0, h]
    lgb = lg_ref[1, h]
    pos = lax.broadcasted_iota(jnp.int32, (t, 1), 0).astype(F32)
    chunk_len = jnp.full((1, RET_DV), float(t), F32)
    k = k_ref[...].astype(F32)
    v = v_ref[...]

    @pl.when(i == 0)
    def _():
        s_ref[...] = jnp.zeros_like(s_ref)

    @pl.when(sweep == 0)
    def _():
        n = nc - 1 - i
        rstore_ref[n] = s_ref[...]
        kw = (k * jnp.exp(lgb * pos)).astype(BF16)
        s_ref[...] = s_ref[...] * jnp.exp(lgb * chunk_len) + _dot_tn(kw, v)

    @pl.when(sweep == 1)
    def _():
        q = q_ref[...].astype(F32) * (RET_DK ** -0.5)
        ri = lax.broadcasted_iota(jnp.int32, (t, t), 0)
        ci = lax.broadcasted_iota(jnp.int32, (t, t), 1)
        diff = (ri - ci).astype(F32)
        decay = (jnp.where(diff >= 0, jnp.exp(lgf * jnp.maximum(diff, 0.0)), 0.0)
                 + jnp.where(diff <= 0, jnp.exp(lgb * jnp.maximum(-diff, 0.0)), 0.0))
        scores = _dot_nt(q.astype(BF16), k_ref[...]) * decay
        y = _dot(scores.astype(BF16), v)
        qf = (q * jnp.exp(lgf * (pos + 1.0))).astype(BF16)
        qb = (q * jnp.exp(lgb * (t - pos))).astype(BF16)
        y = y + _dot(qf, s_ref[...].astype(BF16)) + _dot(qb, rstore_ref[i].astype(BF16))
        kw = (k * jnp.exp(lgf * (t - 1.0 - pos))).astype(BF16)
        s_ref[...] = s_ref[...] * jnp.exp(lgf * chunk_len) + _dot_tn(kw, v)
        y = y * lax.rsqrt(jnp.mean(y * y, axis=-1, keepdims=True) + EPS)
        gt = g_ref[...].astype(F32)
        o_ref[...] = (gt * jax.nn.sigmoid(gt) * y).astype(o_ref.dtype)


def retention_mixer(proj, col0, log_decay, t=CHUNK):
    seq = proj.shape[0]
    nc = seq // t
    qb = col0 // RET_DK
    kb = qb + RET_H
    vb = (col0 + 2 * RET_H * RET_DK) // RET_DV
    gb = vb + RET_H

    def rows(sweep, i):
        return sweep * i + (1 - sweep) * (nc - 1 - i)

    return pl.pallas_call(
        functools.partial(_retention_kernel, nc=nc),
        grid=(RET_H, 2, nc),
        in_specs=[pl.BlockSpec(memory_space=pltpu.SMEM),
                  pl.BlockSpec((t, RET_DK), lambda h, s, i: (s * i, qb + h)),
                  pl.BlockSpec((t, RET_DK), lambda h, s, i: (rows(s, i), kb + h)),
                  pl.BlockSpec((t, RET_DV), lambda h, s, i: (rows(s, i), vb + h)),
                  pl.BlockSpec((t, RET_DV), lambda h, s, i: (s * i, gb + h))],
        out_specs=pl.BlockSpec((t, RET_DV), lambda h, s, i: (s * i, h)),
        out_shape=jax.ShapeDtypeStruct((seq, RET_H * RET_DV), BF16),
        scratch_shapes=[pltpu.VMEM((nc, RET_DK, RET_DV), F32), pltpu.VMEM((RET_DK, RET_DV), F32)],
        compiler_params=_cparams("arbitrary", "arbitrary", "arbitrary"),
        name="retention",
    )(log_decay, proj, proj, proj, proj)


def _log_sigmoid(x):
    return jnp.minimum(x, 0.0) - jnp.log(1.0 + jnp.exp(-jnp.abs(x)))


def _mlstm_gates(gc_ref, gr_ref, bias_ref, h, direction):
    bi = bias_ref[direction * 2 * ML_H + h]
    bf = bias_ref[direction * 2 * ML_H + ML_H + h]
    a = 2 * direction
    ig_c = gc_ref[0, :, a:a + 1] + bi
    lf_c = _log_sigmoid(gc_ref[0, :, a + 1:a + 2] + bf)
    ig_r = gr_ref[0, a:a + 1, :] + bi
    lf_r = _log_sigmoid(gr_ref[0, a + 1:a + 2, :] + bf)
    return ig_c, lf_c, ig_r, lf_r


def _cumsum_cols_rows(lf_c, lf_r, backward):
    t = lf_c.shape[0]
    ri = lax.broadcasted_iota(jnp.int32, (t, t), 0)
    ci = lax.broadcasted_iota(jnp.int32, (t, t), 1)
    tri = ((ri <= ci) if backward else (ri >= ci)).astype(F32)
    cum_c = jnp.dot(tri, jnp.broadcast_to(lf_c, (t, LANES)), preferred_element_type=F32, precision=HIGHEST)[:, 0:1]
    cum_r = lax.dot_general(jnp.broadcast_to(lf_r, (8, t)), tri, (((1,), (1,)), ((), ())),
                            preferred_element_type=F32, precision=HIGHEST)[0:1, :]
    return cum_c, cum_r


def _mlstm_state_step(k, v, ig_c, cum_c, total, c_ref, n_ref, m_ref):
    a = total - cum_c + ig_c
    m_loc = jnp.max(a, axis=0, keepdims=True)
    kw = k * jnp.exp(a - m_loc)
    kv = _dot_tn(kw.astype(BF16), v)
    ksum = jnp.sum(kw, axis=0, keepdims=True)
    m_old = m_ref[0:1, 0:1]
    m_new = jnp.maximum(total + m_old, m_loc)
    sp = jnp.exp(total + m_old - m_new)
    sc = jnp.exp(m_loc - m_new)
    c_ref[...] = sp * c_ref[...] + sc * kv
    n_ref[...] = sp * n_ref[...] + sc * jnp.broadcast_to(ksum, n_ref.shape)
    m_ref[...] = jnp.broadcast_to(m_new, m_ref.shape)


def _mlstm_output(qk, q, v, ig_r, cum_c, cum_r, c_prev, n_prev, m_prev, backward):
    t = q.shape[0]
    ri = lax.broadcasted_iota(jnp.int32, (t, t), 0)
    ci = lax.broadcasted_iota(jnp.int32, (t, t), 1)
    keep = (ri <= ci) if backward else (ri >= ci)
    dlog = jnp.where(keep, cum_c - cum_r + ig_r, -jnp.inf)
    inter = cum_c + m_prev
    m_t = jnp.maximum(inter, jnp.max(dlog, axis=-1, keepdims=True))
    s = qk * jnp.exp(dlog - m_t)
    wi = jnp.exp(inter - m_t)
    num = _dot(s.astype(BF16), v) + wi * _dot(q.astype(BF16), c_prev.astype(BF16))
    den = jnp.sum(s, axis=-1, keepdims=True) + wi * jnp.sum(q * n_prev, axis=-1, keepdims=True)
    return num / jnp.maximum(jnp.abs(den), jnp.exp(-m_t))


def _mlstm_kernel(bias_ref, q_ref, k_ref, v_ref, o_ref, gc_ref, gr_ref, gain_ref, out_ref,
                  cstore_ref, nstore_ref, mstore_ref, c_ref, n_ref, m_ref, *, nc):
    h = pl.program_id(0)
    sweep = pl.program_id(1)
    i = pl.program_id(2)
    k = k_ref[...].astype(F32) * (ML_DK ** -0.5)
    v = v_ref[...]

    @pl.when(i == 0)
    def _():
        c_ref[...] = jnp.zeros_like(c_ref)
        n_ref[...] = jnp.zeros_like(n_ref)
        m_ref[...] = jnp.zeros_like(m_ref)

    @pl.when(sweep == 0)
    def _():
        n = nc - 1 - i
        cstore_ref[n] = c_ref[...]
        nstore_ref[n] = n_ref[...]
        mstore_ref[n] = m_ref[...]
        ig_c, lf_c, _, lf_r = _mlstm_gates(gc_ref, gr_ref, bias_ref, h, 1)
        cum_c, _ = _cumsum_cols_rows(lf_c, lf_r, True)
        total = jnp.sum(lf_c, axis=0, keepdims=True)
        _mlstm_state_step(k, v, ig_c, cum_c, total, c_ref, n_ref, m_ref)

    @pl.when(sweep == 1)
    def _():
        q = q_ref[...].astype(F32)
        qk = _dot_nt(q_ref[...], k.astype(BF16))
        ig_c, lf_c, ig_r, lf_r = _mlstm_gates(gc_ref, gr_ref, bias_ref, h, 0)
        cum_c, cum_r = _cumsum_cols_rows(lf_c, lf_r, False)
        hf = _mlstm_output(qk, q, v, ig_r, cum_c, cum_r, c_ref[...], n_ref[0:1, :], m_ref[0:1, 0:1], False)
        total = jnp.sum(lf_c, axis=0, keepdims=True)
        _mlstm_state_step(k, v, ig_c, cum_c, total, c_ref, n_ref, m_ref)
        _, lb_c, igb_r, lb_r = _mlstm_gates(gc_ref, gr_ref, bias_ref, h, 1)
        cumb_c, cumb_r = _cumsum_cols_rows(lb_c, lb_r, True)
        hb = _mlstm_output(qk, q, v, igb_r, cumb_c, cumb_r, cstore_ref[i], nstore_ref[i][0:1, :],
                           mstore_ref[i][0:1, 0:1], True)
        y = hf + hb
        y = y * lax.rsqrt(jnp.mean(y * y, axis=-1, keepdims=True) + EPS) * gain_ref[...]
        out_ref[...] = (jax.nn.sigmoid(o_ref[...].astype(F32)) * y).astype(out_ref.dtype)


def mlstm_mixer(proj, gates, col0, gate_bias, norm_gain, t=CHUNK):
    seq = proj.shape[0]
    nc = seq // t
    qb = col0 // ML_DK
    kb = qb + ML_H
    vb = (col0 + 2 * ML_H * ML_DK) // ML_DV
    ob = vb + ML_H
    g = gates[:, :4 * ML_H].reshape(seq, 2, 2, ML_H)
    g = jnp.transpose(g, (3, 0, 1, 2)).reshape(ML_H, seq, 4)
    g_cols = g
    g_rows = jnp.transpose(g, (0, 2, 1))

    def rows(sweep, i):
        return sweep * i + (1 - sweep) * (nc - 1 - i)

    return pl.pallas_call(
        functools.partial(_mlstm_kernel, nc=nc),
        grid=(ML_H, 2, nc),
        in_specs=[pl.BlockSpec(memory_space=pltpu.SMEM),
                  pl.BlockSpec((t, ML_DK), lambda h, s, i: (s * i, qb + h)),
                  pl.BlockSpec((t, ML_DK), lambda h, s, i: (rows(s, i), kb + h)),
                  pl.BlockSpec((t, ML_DV), lambda h, s, i: (rows(s, i), vb + h)),
                  pl.BlockSpec((t, ML_DV), lambda h, s, i: (s * i, ob + h)),
                  pl.BlockSpec((1, t, 4), lambda h, s, i: (h, rows(s, i), 0)),
                  pl.BlockSpec((1, 4, t), lambda h, s, i: (h, 0, rows(s, i))),
                  pl.BlockSpec((1, ML_DV), lambda h, s, i: (0, h))],
        out_specs=pl.BlockSpec((t, ML_DV), lambda h, s, i: (s * i, h)),
        out_shape=jax.ShapeDtypeStruct((seq, ML_H * ML_DV), BF16),
        scratch_shapes=[pltpu.VMEM((nc, ML_DK, ML_DV), F32), pltpu.VMEM((nc, 8, ML_DK), F32),
                        pltpu.VMEM((nc, 8, LANES), F32), pltpu.VMEM((ML_DK, ML_DV), F32),
                        pltpu.VMEM((8, ML_DK), F32), pltpu.VMEM((8, LANES), F32)],
        compiler_params=_cparams("arbitrary", "arbitrary", "arbitrary"),
        name="mlstm",
    )(gate_bias, proj, proj, proj, proj, g_cols, g_rows, norm_gain.reshape(1, ML_H * ML_DV))


PERM_ROWS = 256
CLASS_RUN = 16


def _group_permutation(dilation):
    run = CLASS_RUN * dilation
    new = jnp.arange(PERM_ROWS, dtype=jnp.int32)
    within = new % run
    src = (new // run) * run + (within % CLASS_RUN) * dilation + within // CLASS_RUN
    return (src[:, None] == jnp.arange(PERM_ROWS, dtype=jnp.int32)[None, :]).astype(BF16)


def _attention_prep_kernel(gain_ref, p4_ref, p16_ref, x_ref, o1_ref, o4_ref, o16_ref):
    part = pl.program_id(1)

    @pl.when(part < 2)
    def _():
        gain = gain_ref[0]
        for h in range(ATT_H):
            cols = slice(h * ATT_DH, (h + 1) * ATT_DH)
            x = x_ref[:, cols].astype(F32)
            o1_ref[:, cols] = (x * lax.rsqrt(jnp.mean(x * x, axis=-1, keepdims=True) + EPS) * gain).astype(BF16)

    @pl.when(part == 2)
    def _():
        o1_ref[...] = x_ref[...]

    for s in range(x_ref.shape[0] // PERM_ROWS):
        rows = slice(s * PERM_ROWS, (s + 1) * PERM_ROWS)
        x = o1_ref[rows, :]
        o4_ref[rows, :] = _dot(p4_ref[...], x).astype(BF16)
        o16_ref[rows, :] = _dot(p16_ref[...], x).astype(BF16)


def attention_prep(proj, col0, qk_gain, tr=512):
    seq = proj.shape[0]
    cb = col0 // W_GROUP
    gains = jnp.stack([qk_gain[0] * (ATT_DH ** -0.5), qk_gain[1], jnp.ones_like(qk_gain[0])]).reshape(3, 1, ATT_DH)
    out = jax.ShapeDtypeStruct((seq, 3 * W_GROUP), BF16)
    ospec = pl.BlockSpec((tr, W_GROUP), lambda i, j: (i, j))
    pspec = pl.BlockSpec((PERM_ROWS, PERM_ROWS), lambda i, j: (0, 0))
    return pl.pallas_call(
        _attention_prep_kernel,
        grid=(seq // tr, 3),
        in_specs=[pl.BlockSpec((1, 1, ATT_DH), lambda i, j: (j, 0, 0)), pspec, pspec,
                  pl.BlockSpec((tr, W_GROUP), lambda i, j: (i, cb + j))],
        out_specs=[ospec, ospec, ospec],
        out_shape=[out, out, out],
        compiler_params=_cparams("parallel", "arbitrary"),
        name="attention_prep",
    )(gains.astype(F32), _group_permutation(4), _group_permutation(16), proj)


def _band_attention_kernel(q_ref, kp_ref, kc_ref, kn_ref, vp_ref, vc_ref, vn_ref, o_ref, lse_ref,
                           *, dilation, nblk):
    i = pl.program_id(1)
    tq = q_ref.shape[0] * CLASS_RUN
    hs = ATT_HALF_STEPS
    tk = tq + 2 * hs
    ri = lax.broadcasted_iota(jnp.int32, (tq, tk), 0)
    ci = lax.broadcasted_iota(jnp.int32, (tq, tk), 1)
    off = ci - hs - ri
    first_col = jnp.where(i > 0, 0, hs)
    end_col = jnp.where(i < nblk - 1, tk, tq + hs)
    valid = (jnp.abs(off) <= hs) & (ci >= first_col) & (ci < end_col)
    dist = (jnp.abs(off) * dilation).astype(F32)
    lane = lax.broadcasted_iota(jnp.int32, (tq, LANES), 1)
    lse_all = jnp.zeros((tq, LANES), F32)

    def rows(ref, cols):
        x = ref[:, :, cols]
        return x.reshape(x.shape[0] * CLASS_RUN, x.shape[2])

    for h in range(ATT_H):
        cols = slice(h * ATT_DH, (h + 1) * ATT_DH)
        slope = 2.0 ** (-8.0 * (h + 1) / ATT_H)
        kk = jnp.concatenate([rows(kp_ref, cols), rows(kc_ref, cols), rows(kn_ref, cols)], axis=0)
        vv = jnp.concatenate([rows(vp_ref, cols), rows(vc_ref, cols), rows(vn_ref, cols)], axis=0)
        s = _dot_nt(rows(q_ref, cols), kk) - slope * dist
        s = jnp.where(valid, s, NEG)
        m = jnp.max(s, axis=-1, keepdims=True)
        p = jnp.exp(s - m)
        den = jnp.sum(p, axis=-1, keepdims=True)
        o = _dot(p.astype(BF16), vv) / den
        o_ref[:, :, cols] = o.astype(o_ref.dtype).reshape(tq // CLASS_RUN, CLASS_RUN, ATT_DH)
        lse_all = jnp.where(lane == h, m + jnp.log(den), lse_all)
    lse_ref[...] = lse_all.reshape(tq // CLASS_RUN, CLASS_RUN, LANES)


def band_attention(qkv, dilation, tq=256):
    seq = qkv.shape[0]
    n = seq // dilation
    tq = min(tq, n)
    nblk = n // tq
    hs = ATT_HALF_STEPS
    runs = n // CLASS_RUN
    tr = tq // CLASS_RUN
    hr = hs // CLASS_RUN
    ratio = tq // hs
    last_halo = n // hs - 1
    view = qkv.reshape(runs, dilation, CLASS_RUN, 3 * W_GROUP)

    def cur(part):
        return pl.BlockSpec((tr, None, CLASS_RUN, W_GROUP), lambda r, i: (i, r, 0, part))

    def prev(part):
        return pl.BlockSpec((hr, None, CLASS_RUN, W_GROUP), lambda r, i: (jnp.maximum(i * ratio - 1, 0), r, 0, part))

    def nxt(part):
        return pl.BlockSpec((hr, None, CLASS_RUN, W_GROUP),
                            lambda r, i: (jnp.minimum((i + 1) * ratio, last_halo), r, 0, part))

    o, lse = pl.pallas_call(
        functools.partial(_band_attention_kernel, dilation=dilation, nblk=nblk),
        grid=(dilation, nblk),
        in_specs=[cur(0), prev(1), cur(1), nxt(1), prev(2), cur(2), nxt(2)],
        out_specs=[pl.BlockSpec((tr, None, CLASS_RUN, W_GROUP), lambda r, i: (i, r, 0, 0)),
                   pl.BlockSpec((tr, None, CLASS_RUN, LANES), lambda r, i: (i, r, 0, 0))],
        out_shape=[jax.ShapeDtypeStruct((runs, dilation, CLASS_RUN, W_GROUP), BF16),
                   jax.ShapeDtypeStruct((runs, dilation, CLASS_RUN, LANES), F32)],
        compiler_params=_cparams("parallel", "arbitrary"),
        name=f"band_attention_d{dilation}",
    )(view, view, view, view, view, view, view)
    return o.reshape(seq, W_GROUP), lse.reshape(seq, LANES)


def _merge_branches_kernel(q4_ref, q16_ref, o1_ref, o4_ref, o16_ref, l1_ref, l4_ref, l16_ref, out_ref):
    def ungroup(qt, x):
        return _dot(qt, x)

    def ungroup_f32(qt, x):
        hi = x.astype(BF16)
        lo = (x - hi.astype(F32)).astype(BF16)
        return _dot(qt, hi) + _dot(qt, lo)

    for s in range(o1_ref.shape[0] // PERM_ROWS):
        rows = slice(s * PERM_ROWS, (s + 1) * PERM_ROWS)
        q4, q16 = q4_ref[...], q16_ref[...]
        l1 = l1_ref[rows, :]
        l2 = ungroup_f32(q4, l4_ref[rows, :])
        l3 = ungroup_f32(q16, l16_ref[rows, :])
        m = jnp.maximum(jnp.maximum(l1, l2), l3)
        e1, e2, e3 = jnp.exp(l1 - m), jnp.exp(l2 - m), jnp.exp(l3 - m)
        inv = 1.0 / (e1 + e2 + e3)
        w1, w2, w3 = e1 * inv, e2 * inv, e3 * inv
        o2 = ungroup(q4, o4_ref[rows, :])
        o3 = ungroup(q16, o16_ref[rows, :])
        for h in range(ATT_H):
            cols = slice(h * ATT_DH, (h + 1) * ATT_DH)
            out_ref[rows, cols] = (w1[:, h:h + 1] * o1_ref[rows, cols].astype(F32)
                                   + w2[:, h:h + 1] * o2[:, cols]
                                   + w3[:, h:h + 1] * o3[:, cols]).astype(out_ref.dtype)


def merge_branches(outs, lses, tr=512):
    seq = outs[0].shape[0]
    ospec = pl.BlockSpec((tr, W_GROUP), lambda i: (i, 0))
    lspec = pl.BlockSpec((tr, LANES), lambda i: (i, 0))
    pspec = pl.BlockSpec((PERM_ROWS, PERM_ROWS), lambda i: (0, 0))
    return pl.pallas_call(
        _merge_branches_kernel,
        grid=(seq // tr,),
        in_specs=[pspec, pspec, ospec, ospec, ospec, lspec, lspec, lspec],
        out_specs=ospec,
        out_shape=jax.ShapeDtypeStruct((seq, W_GROUP), BF16),
        compiler_params=_cparams("parallel"),
        name="attention_merge",
    )(_group_permutation(4).T, _group_permutation(16).T, *outs, *lses)


def dilated_attention(proj, col0, qk_gain):
    grouped = attention_prep(proj, col0, qk_gain)
    outs, lses = [], []
    for d, qkv in zip(ATT_DILATIONS, grouped):
        o, lse = band_attention(qkv, d)
        outs.append(o)
        lses.append(lse)
    return merge_branches(outs, lses)


def kernel(x, c, ada_w, ada_b, ada_table, w_in, w_out, hy_short, hy_w1, hy_b1, hy_w2, hy_b2, hy_w3, hy_bias,
           ret_decay, att_qk_gain, ml_gate_bias, ml_norm_gain, ffn_w1, ffn_w3, ffn_w2):
    batch, seq, d_model = x.shape
    depth = w_in.shape[0]
    d_main = 12 * W_GROUP
    hidden = ffn_w1.shape[2]
    hidden_pad = -(-hidden // 1024) * 1024

    tabs = _dft_tables(seq)
    consts = _filter_positions(seq)
    mod_shared = ada_modulation(c, ada_w, ada_b)

    rows = x.reshape(batch * seq, d_model)
    outs = []
    for b in range(batch):
        xb = rows[b * seq:(b + 1) * seq]
        for l in range(depth):
            mod = (mod_shared[b:b + 1] + ada_table[l].reshape(1, -1)).reshape(6, d_model)
            sh1, sc1, g1, sh2, sc2, g2 = (mod[i:i + 1] for i in range(6))
            w_main = w_in[l, :, :d_main].astype(BF16)
            w_gate = jnp.pad(w_in[l, :, d_main:], ((0, 0), (0, LANES - (w_in.shape[2] - d_main)))).astype(BF16)
            proj, gates = norm_proj(xb, 1.0 + sc1, sh1, w_main, w_gate)

            kf = hyena_filter_spectra(seq, tabs, consts, hy_w1[l], hy_b1[l], hy_w2[l], hy_b2[l], hy_w3[l])
            y_a = hyena_mixer(proj, tabs, kf, hy_short[l], hy_bias[l])
            y_b = retention_mixer(proj, 3 * W_GROUP, jax.nn.log_sigmoid(ret_decay[l].astype(F32)))
            y_c = dilated_attention(proj, 6 * W_GROUP, att_qk_gain[l])
            y_d = mlstm_mixer(proj, gates, 9 * W_GROUP, ml_gate_bias[l], ml_norm_gain[l])
            y = jnp.concatenate([y_a.astype(BF16), y_b, y_c, y_d], axis=-1)
            xb = mm_residual(y, w_out[l].astype(BF16), xb, g1, tm=1024, tn=512, tk=y.shape[1])

            pad = ((0, 0), (0, hidden_pad - hidden))
            w1 = jnp.pad(ffn_w1[l], pad).astype(BF16)
            w3 = jnp.pad(ffn_w3[l], pad).astype(BF16)
            w2 = jnp.pad(ffn_w2[l], (pad[1], pad[0])).astype(BF16)
            u = norm_swiglu(xb, 1.0 + sc2, sh2, w1, w3)
            xb = mm_residual(u, w2, xb, g2, tm=1024, tn=1024, tk=hidden_pad // 4)
        outs.append(xb)
    return jnp.concatenate(outs, 0).reshape(batch, seq, d_model)
```

```python
import functools
import math

import numpy as np
import jax
import jax.numpy as jnp
from jax import lax
from jax.experimental import pallas as pl
from jax.experimental.pallas import tpu as pltpu

F32 = jnp.float32
BF16 = jnp.bfloat16
HIGHEST = lax.Precision.HIGHEST

EPS = 1e-6
NEG = -1e30

V7X_VMEM_LIMIT_BYTES = 56 * 1024 * 1024
LANES = 128

W_GROUP = 1024
RET_H, RET_DK, RET_DV = 4, 128, 256
ATT_H, ATT_DH = 8, 128
ATT_HALF_STEPS = 64
ATT_DILATIONS = (1, 4, 16)
ML_H, ML_DK, ML_DV = 4, 128, 256
HY_BANDS = 16
HY_W = W_GROUP
HY_FILTER_WIDTH = 64
HY_TARGET, HY_SHORT_PCT, HY_LONG_PCT = 1e-2, 0.3, 1.5
HY_FEAT_PAD = 40
DFT_N2 = 128
CHUNK = 256


def _cparams(*sem):
    return pltpu.CompilerParams(dimension_semantics=sem, vmem_limit_bytes=V7X_VMEM_LIMIT_BYTES)


def _dot(a, b):
    return jnp.dot(a, b, preferred_element_type=F32)


def _dot_nt(a, b):
    return lax.dot_general(a, b, (((1,), (1,)), ((), ())), preferred_element_type=F32)


def _dot_tn(a, b):
    return lax.dot_general(a, b, (((0,), (0,)), ((), ())), preferred_element_type=F32)


def _ada_kernel(c_ref, w_ref, b_ref, o_ref):
    c = c_ref[...]
    s = c * jax.nn.sigmoid(c)
    o_ref[...] = jnp.dot(s, w_ref[...], preferred_element_type=F32, precision=HIGHEST) + b_ref[...]


def ada_modulation(c, ada_w, ada_b):
    d, n = ada_w.shape
    tn = 512
    c8 = jnp.broadcast_to(c.reshape(1, d), (8, d))
    out = pl.pallas_call(
        _ada_kernel,
        grid=(n // tn,),
        in_specs=[pl.BlockSpec((8, d), lambda j: (0, 0)),
                  pl.BlockSpec((d, tn), lambda j: (0, j)),
                  pl.BlockSpec((1, tn), lambda j: (0, j))],
        out_specs=pl.BlockSpec((8, tn), lambda j: (0, j)),
        out_shape=jax.ShapeDtypeStruct((8, n), F32),
        compiler_params=_cparams("parallel"),
        name="ada_modulation",
    )(c8, ada_w, ada_b.reshape(1, n))
    return out[0:1]


NORM_ROWS = 64


def _normalise_into(x_ref, sc_ref, sh_ref, h_ref):
    tm = x_ref.shape[0]
    sc = sc_ref[...]
    sh = sh_ref[...]

    def body(r, carry):
        rows = pl.ds(pl.multiple_of(r * NORM_ROWS, NORM_ROWS), NORM_ROWS)
        x = x_ref[rows, :]
        ms = jnp.mean(x * x, axis=-1, keepdims=True)
        h_ref[rows, :] = (x * lax.rsqrt(ms + EPS) * sc + sh).astype(h_ref.dtype)
        return carry

    lax.fori_loop(0, tm // NORM_ROWS, body, 0)


def _norm_proj_kernel(x_ref, sc_ref, sh_ref, w_ref, wg_ref, o_ref, g_ref, h_ref):
    @pl.when(pl.program_id(1) == 0)
    def _():
        _normalise_into(x_ref, sc_ref, sh_ref, h_ref)
        g_ref[...] = _dot(h_ref[...], wg_ref[...])

    o_ref[...] = _dot(h_ref[...], w_ref[...]).astype(o_ref.dtype)


def norm_proj(x, scale1p, shift, w_stack, layer, n, wg, tm=512, tn=1024):
    m, d = x.shape
    ng = wg.shape[1]
    return pl.pallas_call(
        _norm_proj_kernel,
        grid=(m // tm, n // tn),
        in_specs=[pl.BlockSpec((tm, d), lambda i, j: (i, 0)),
                  pl.BlockSpec((1, d), lambda i, j: (0, 0)),
                  pl.BlockSpec((1, d), lambda i, j: (0, 0)),
                  pl.BlockSpec((None, d, tn), lambda i, j: (layer, 0, j)),
                  pl.BlockSpec((d, ng), lambda i, j: (0, 0))],
        out_specs=[pl.BlockSpec((tm, tn), lambda i, j: (i, j)),
                   pl.BlockSpec((tm, ng), lambda i, j: (i, 0))],
        out_shape=[jax.ShapeDtypeStruct((m, n), BF16), jax.ShapeDtypeStruct((m, ng), F32)],
        scratch_shapes=[pltpu.VMEM((tm, d), BF16)],
        compiler_params=_cparams("parallel", "arbitrary"),
        name="norm_proj",
    )(x, scale1p, shift, w_stack, wg)


def _norm_swiglu_kernel(x_ref, sc_ref, sh_ref, w1_ref, w3_ref, o_ref, h_ref):
    @pl.when(pl.program_id(1) == 0)
    def _():
        _normalise_into(x_ref, sc_ref, sh_ref, h_ref)

    h = h_ref[...]
    a = _dot(h, w1_ref[...])
    b = _dot(h, w3_ref[...])
    o_ref[...] = (a * jax.nn.sigmoid(a) * b).astype(o_ref.dtype)


def norm_swiglu(x, scale1p, shift, w1_stack, w3_stack, layer, tm=512, tn=512):
    m, d = x.shape
    n = w1_stack.shape[2]
    return pl.pallas_call(
        _norm_swiglu_kernel,
        grid=(m // tm, n // tn),
        in_specs=[pl.BlockSpec((tm, d), lambda i, j: (i, 0)),
                  pl.BlockSpec((1, d), lambda i, j: (0, 0)),
                  pl.BlockSpec((1, d), lambda i, j: (0, 0)),
                  pl.BlockSpec((None, d, tn), lambda i, j: (layer, 0, j)),
                  pl.BlockSpec((None, d, tn), lambda i, j: (layer, 0, j))],
        out_specs=pl.BlockSpec((tm, tn), lambda i, j: (i, j)),
        out_shape=jax.ShapeDtypeStruct((m, n), BF16),
        scratch_shapes=[pltpu.VMEM((tm, d), BF16)],
        compiler_params=_cparams("parallel", "arbitrary"),
        name="norm_swiglu",
    )(x, scale1p, shift, w1_stack, w3_stack)


def _mm_residual_kernel(y_ref, w_ref, x_ref, g_ref, o_ref, acc_ref, *, nk):
    k = pl.program_id(2)
    part = _dot(y_ref[...], w_ref[...])

    @pl.when(k == 0)
    def _():
        acc_ref[...] = part

    @pl.when(k > 0)
    def _():
        acc_ref[...] += part

    @pl.when(k == nk - 1)
    def _():
        o_ref[...] = x_ref[...] + g_ref[...] * acc_ref[...]


def _mm_residual_1k_kernel(y_ref, w_ref, x_ref, g_ref, o_ref):
    o_ref[...] = x_ref[...] + g_ref[...] * _dot(y_ref[...], w_ref[...])


def mm_residual(y, w_stack, layer, x, gate, tm, tn, tk):
    m, kk = y.shape
    n = w_stack.shape[2]
    nk = kk // tk
    if nk == 1:
        return pl.pallas_call(
            _mm_residual_1k_kernel,
            grid=(m // tm, n // tn),
            in_specs=[pl.BlockSpec((tm, kk), lambda i, j: (i, 0)),
                      pl.BlockSpec((None, kk, tn), lambda i, j: (layer, 0, j)),
                      pl.BlockSpec((tm, tn), lambda i, j: (i, j)),
                      pl.BlockSpec((1, tn), lambda i, j: (0, j))],
            out_specs=pl.BlockSpec((tm, tn), lambda i, j: (i, j)),
            out_shape=jax.ShapeDtypeStruct((m, n), F32),
            compiler_params=_cparams("parallel", "arbitrary"),
            name="mm_residual",
        )(y, w_stack, x, gate)
    return pl.pallas_call(
        functools.partial(_mm_residual_kernel, nk=nk),
        grid=(m // tm, n // tn, nk),
        in_specs=[pl.BlockSpec((tm, tk), lambda i, j, k: (i, k)),
                  pl.BlockSpec((None, tk, tn), lambda i, j, k: (layer, k, j)),
                  pl.BlockSpec((tm, tn), lambda i, j, k: (i, j)),
                  pl.BlockSpec((1, tn), lambda i, j, k: (0, j))],
        out_specs=pl.BlockSpec((tm, tn), lambda i, j, k: (i, j)),
        out_shape=jax.ShapeDtypeStruct((m, n), F32),
        scratch_shapes=[pltpu.VMEM((tm, tn), F32)],
        compiler_params=_cparams("parallel", "arbitrary", "arbitrary"),
        name="mm_residual_ksplit",
    )(y, w_stack, x, gate)


def _dft_tables(seq_len):
    n = 2 * seq_len
    n2 = DFT_N2
    n1 = n // n2
    half = n1 // 2
    kb = half + 1
    kbp = -(-kb // 8) * 8
    k1 = jnp.arange(kbp, dtype=jnp.int32)
    live = (k1 <= half)
    col = jnp.arange(n1, dtype=jnp.int32)
    ph = (2.0 * math.pi / n1) * ((k1[:, None] * col[None, :]) % n1).astype(F32)
    wf = jnp.stack([jnp.cos(ph), -jnp.sin(ph)], axis=0) * live[None, :, None]
    wf = wf.reshape(2 * kbp, n1)
    a = jnp.arange(n2, dtype=jnp.int32)
    mm = (a[None, None, :] * (k1[:, None, None] + n1 * a[None, :, None])) % n
    th = (2.0 * math.pi / n) * mm.astype(F32)
    c, s = jnp.cos(th), jnp.sin(th)
    tf = jnp.concatenate([jnp.concatenate([c, s], 2), jnp.concatenate([-s, c], 2)], 1)
    tf = tf * live[:, None, None]
    ti = jnp.transpose(tf, (0, 2, 1))
    wt = jnp.where((k1 == 0) | (k1 == half), 1.0, 2.0) * live / n
    row = jnp.arange(half, dtype=jnp.int32)
    ph2 = (2.0 * math.pi / n1) * ((row[:, None] * k1[None, :]) % n1).astype(F32)
    wi = jnp.stack([wt * jnp.cos(ph2), -wt * jnp.sin(ph2)], axis=1).reshape(half, 2 * kbp)
    return dict(wf_data=wf[:, :half].astype(BF16), wf_filt=wf.astype(BF16), tf=tf.astype(BF16),
                ti=ti.astype(BF16), wi=wi.astype(BF16), kbp=kbp, n1=n1)


INNER_STEP = 8


def _outer_dft_kernel(w_ref, x_ref, o_ref):
    kbp = o_ref.shape[1]
    w = w_ref[...]
    for j in range(x_ref.shape[0]):
        a = _dot(w, x_ref[j].astype(BF16))
        o_ref[0, :, j, :] = a[:kbp]
        o_ref[1, :, j, :] = a[kbp:]


def outer_dft(w, x, tc=1024):
    r2, k = w.shape
    inner, _, ch = x.shape
    return pl.pallas_call(
        _outer_dft_kernel,
        grid=(inner // INNER_STEP, ch // tc),
        in_specs=[pl.BlockSpec((r2, k), lambda i, j: (0, 0)),
                  pl.BlockSpec((INNER_STEP, k, tc), lambda i, j: (i, 0, j))],
        out_specs=pl.BlockSpec((2, r2 // 2, INNER_STEP, tc), lambda i, j: (0, 0, i, j)),
        out_shape=jax.ShapeDtypeStruct((2, r2 // 2, inner, ch), F32),
        compiler_params=_cparams("parallel", "parallel"),
        name="hyena_outer_dft",
    )(w, x)


def _filter_spectrum_kernel(tf_ref, a_ref, inv_ref, o_ref):
    inv = inv_ref[...]
    for b in range(tf_ref.shape[0]):
        x = jnp.concatenate([a_ref[0, b], a_ref[1, b]], axis=0).astype(BF16)
        o_ref[b] = (_dot(tf_ref[b], x) * inv).astype(o_ref.dtype)


def filter_spectrum(tf, a, inv_l1, kb=4, tc=1024):
    _, kbp, inner, ch = a.shape
    r = 2 * inner
    return pl.pallas_call(
        _filter_spectrum_kernel,
        grid=(kbp // kb, ch // tc),
        in_specs=[pl.BlockSpec((kb, r, r), lambda i, j: (i, 0, 0)),
                  pl.BlockSpec((2, kb, inner, tc), lambda i, j: (0, i, 0, j)),
                  pl.BlockSpec((1, tc), lambda i, j: (0, j))],
        out_specs=pl.BlockSpec((kb, r, tc), lambda i, j: (i, 0, j)),
        out_shape=jax.ShapeDtypeStruct((kbp, r, ch), BF16),
        compiler_params=_cparams("parallel", "arbitrary"),
        name="hyena_filter_spectrum",
    )(tf, a, inv_l1)


def _spectral_conv_kernel(tf_ref, ti_ref, a_ref, k_ref, gre_ref, gim_ref):
    half = a_ref.shape[2]
    for b in range(tf_ref.shape[0]):
        x = jnp.concatenate([a_ref[0, b], a_ref[1, b]], axis=0).astype(BF16)
        z = _dot(tf_ref[b], x)
        zr, zi = z[:half], z[half:]
        kr = k_ref[b, :half, :].astype(F32)
        ki = k_ref[b, half:, :].astype(F32)
        y = jnp.concatenate([zr * kr - zi * ki, zr * ki + zi * kr], axis=0).astype(BF16)
        g = _dot(ti_ref[b], y)
        gre_ref[:, b, :] = g[:half]
        gim_ref[:, b, :] = g[half:]


def spectral_conv(tf, ti, a, kf, order, tc=512):
    _, kbp, inner, ch = a.shape
    r = 2 * inner
    kb = INNER_STEP
    ncb = ch // tc
    gspec = pl.BlockSpec((inner, kb, tc), lambda i, j: (0, i, j))
    gshape = jax.ShapeDtypeStruct((inner, kbp, ch), F32)
    return pl.pallas_call(
        _spectral_conv_kernel,
        grid=(kbp // kb, ncb),
        in_specs=[pl.BlockSpec((kb, r, r), lambda i, j: (i, 0, 0)),
                  pl.BlockSpec((kb, r, r), lambda i, j: (i, 0, 0)),
                  pl.BlockSpec((2, kb, inner, tc), lambda i, j: (0, i, 0, j)),
                  pl.BlockSpec((kb, r, tc), lambda i, j: (i, 0, order * ncb + j))],
        out_specs=[gspec, gspec],
        out_shape=[gshape, gshape],
        compiler_params=_cparams("parallel", "arbitrary"),
        name="hyena_spectral_conv",
    )(tf, ti, a, kf)


def _outer_inverse_gate_kernel(w_ref, gre_ref, gim_ref, gate_ref, zp_ref, bias_ref, o_ref, *, token_major_out):
    w = w_ref[...]
    bias = bias_ref[...]
    for j in range(gre_ref.shape[0]):
        g = jnp.concatenate([gre_ref[j], gim_ref[j]], axis=0).astype(BF16)
        z = gate_ref[j] * (_dot(w, g) + bias * zp_ref[j])
        if token_major_out:
            o_ref[:, j, :] = z
        else:
            o_ref[j] = z


def outer_inverse_gate(wi, g_re, g_im, gate, z_prev, bias, token_major_out, tc=512):
    r, k2 = wi.shape
    inner, kbp, ch = g_re.shape
    gspec = pl.BlockSpec((INNER_STEP, kbp, tc), lambda i, j: (i, 0, j))
    zspec = pl.BlockSpec((INNER_STEP, r, tc), lambda i, j: (i, 0, j))
    if token_major_out:
        ospec, oshape = pl.BlockSpec((r, INNER_STEP, tc), lambda i, j: (0, i, j)), (r, inner, ch)
    else:
        ospec, oshape = zspec, (inner, r, ch)
    return pl.pallas_call(
        functools.partial(_outer_inverse_gate_kernel, token_major_out=token_major_out),
        grid=(inner // INNER_STEP, ch // tc),
        in_specs=[pl.BlockSpec((r, k2), lambda i, j: (0, 0)), gspec, gspec, zspec, zspec,
                  pl.BlockSpec((1, tc), lambda i, j: (0, j))],
        out_specs=ospec,
        out_shape=jax.ShapeDtypeStruct(oshape, F32),
        compiler_params=_cparams("parallel", "parallel"),
        name="hyena_outer_inverse_gate",
    )(wi, g_re, g_im, gate, z_prev, bias)


def _short_conv_kernel(u_ref, prev_ref, next_ref, w_ref, v_ref, x1_ref, x2_ref):
    tr = u_ref.shape[0]
    ch = v_ref.shape[2]
    row = lax.broadcasted_iota(jnp.int32, (tr, ch), 0)
    for part, o_ref in enumerate((v_ref, x1_ref, x2_ref)):
        cols = slice(part * ch, (part + 1) * ch)
        x = u_ref[:, cols].astype(F32)
        before = jnp.where(row == 0, prev_ref[0, :, cols].astype(F32), pltpu.roll(x, 1, axis=0))
        after = jnp.where(row == tr - 1, next_ref[0, :, cols].astype(F32), pltpu.roll(x, tr - 1, axis=0))
        w = w_ref[:, cols]
        y = before * w[0:1] + x * w[1:2] + after * w[2:3]
        for j in range(tr // DFT_N2):
            o_ref[:, j, :] = y[j * DFT_N2:(j + 1) * DFT_N2]


def short_conv(proj, short_w):
    seq = proj.shape[0]
    width = 3 * HY_W
    tr = INNER_STEP * DFT_N2
    nb = seq // tr
    zero = jnp.zeros((1, width), proj.dtype)
    prev_rows = jnp.concatenate([zero, proj[tr - 1::tr, :width][:-1]], 0).reshape(nb, 1, width)
    next_rows = jnp.concatenate([proj[tr::tr, :width], zero], 0).reshape(nb, 1, width)
    out = jax.ShapeDtypeStruct((DFT_N2, seq // DFT_N2, HY_W), F32)
    return pl.pallas_call(
        _short_conv_kernel,
        grid=(nb,),
        in_specs=[pl.BlockSpec((tr, width), lambda i: (i, 0)),
                  pl.BlockSpec((1, 1, width), lambda i: (i, 0, 0)),
                  pl.BlockSpec((1, 1, width), lambda i: (i, 0, 0)),
                  pl.BlockSpec((3, width), lambda i: (0, 0))],
        out_specs=[pl.BlockSpec((DFT_N2, INNER_STEP, HY_W), lambda i: (0, i, 0))] * 3,
        out_shape=[out, out, out],
        compiler_params=_cparams("parallel"),
        name="hyena_short_conv",
    )(proj, prev_rows, next_rows, short_w)


def _filter_mlp_kernel(z_ref, w1_ref, b1_ref, w2_ref, b2_ref, o_ref):
    hdot = functools.partial(jnp.dot, preferred_element_type=F32, precision=HIGHEST)
    h = jnp.sin(hdot(z_ref[...], w1_ref[...]) + b1_ref[...])
    for i in range(w2_ref.shape[0]):
        h = jnp.sin(hdot(h, w2_ref[i]) + b2_ref[i])
    o_ref[...] = h


def filter_mlp(feat, w1p, b1, w2, b2, tr=1024):
    seq, fp = feat.shape
    wd = w1p.shape[1]
    ni = w2.shape[0]
    tr = min(tr, seq)
    return pl.pallas_call(
        _filter_mlp_kernel,
        grid=(seq // tr,),
        in_specs=[pl.BlockSpec((tr, fp), lambda i: (i, 0)),
                  pl.BlockSpec((fp, wd), lambda i: (0, 0)),
                  pl.BlockSpec((1, wd), lambda i: (0, 0)),
                  pl.BlockSpec((ni, wd, wd), lambda i: (0, 0, 0)),
                  pl.BlockSpec((ni, 1, wd), lambda i: (0, 0, 0))],
        out_specs=pl.BlockSpec((tr, wd), lambda i: (i, 0)),
        out_shape=jax.ShapeDtypeStruct((seq, wd), F32),
        compiler_params=_cparams("parallel"),
        name="hyena_filter_mlp",
    )(feat, w1p, b1.reshape(1, wd), w2, b2.reshape(ni, 1, wd))


def _filter_expand_kernel(h_ref, t_ref, w3_ref, dl_ref, o_ref, s_ref):
    @pl.when(pl.program_id(0) == 0)
    def _():
        s_ref[...] = jnp.zeros_like(s_ref)

    f = jnp.dot(h_ref[...], w3_ref[0], preferred_element_type=F32, precision=HIGHEST)
    f = f * jnp.exp(-t_ref[...] * dl_ref[...])
    tr, ch = f.shape
    for j in range(tr // DFT_N2):
        o_ref[:, j, :] = f[j * DFT_N2:(j + 1) * DFT_N2]
    s_ref[...] += jnp.sum(jnp.abs(f).reshape(tr // 8, 8, ch), axis=0)


def filter_expand(h_full, t_full, w3_halves, deltas2):
    n, wd = h_full.shape
    ch = w3_halves.shape[2]
    tr = INNER_STEP * DFT_N2
    nb = n // tr
    return pl.pallas_call(
        _filter_expand_kernel,
        grid=(nb,),
        in_specs=[pl.BlockSpec((tr, wd), lambda i: (i, 0)),
                  pl.BlockSpec((tr, 1), lambda i: (i, 0)),
                  pl.BlockSpec((1, wd, ch), lambda i: (i // (nb // 2), 0, 0)),
                  pl.BlockSpec((1, ch), lambda i: (0, 0))],
        out_specs=[pl.BlockSpec((DFT_N2, INNER_STEP, ch), lambda i: (0, i, 0)),
                   pl.BlockSpec((8, ch), lambda i: (0, 0))],
        out_shape=[jax.ShapeDtypeStruct((DFT_N2, n // DFT_N2, ch), F32), jax.ShapeDtypeStruct((8, ch), F32)],
        compiler_params=_cparams("arbitrary"),
        name="hyena_filter_expand",
    )(h_full, t_full, w3_halves, deltas2)


def _filter_positions(seq_len):
    pos = np.arange(seq_len, dtype=np.float64)
    t = np.linspace(0.0, 1.0, seq_len)
    ang = (2.0 * math.pi / seq_len) * pos
    freqs = np.linspace(1e-4, HY_BANDS - 1, HY_BANDS)
    feat = np.concatenate([t[:, None], np.cos(ang[:, None] * freqs), -np.sin(ang[:, None] * freqs)], -1)
    feat = np.pad(feat, ((0, 0), (0, HY_FEAT_PAD - feat.shape[1])))
    t_full = np.concatenate([t, np.zeros(1), t[:0:-1]])[:, None]
    max_decay = math.log(HY_TARGET) / HY_SHORT_PCT
    min_decay = math.log(HY_TARGET) / HY_LONG_PCT
    deltas = np.abs(np.linspace(min_decay, max_decay, HY_W))
    return (jnp.asarray(feat, F32), jnp.asarray(t_full, F32),
            jnp.asarray(np.tile(deltas, 2)[None, :], F32))


def hyena_filter_spectra(seq_len, tabs, consts, w1, b1, w2, b2, w3):
    feat, t_full, deltas2 = consts
    w1p = jnp.pad(w1, ((0, HY_FEAT_PAD - w1.shape[0]), (0, 0)))
    h = filter_mlp(feat, w1p, b1, w2, b2)
    h_full = jnp.concatenate([h, jnp.zeros((1, h.shape[1]), F32), jnp.flip(h[1:], 0)], 0)
    w3r = w3.reshape(w3.shape[0], 2, 2, HY_W)
    w3_halves = jnp.transpose(w3r, (2, 0, 1, 3)).reshape(2, w3.shape[0], 2 * HY_W)
    full, sabs = filter_expand(h_full, t_full, w3_halves, deltas2)
    inv_l1 = 1.0 / jnp.sum(sabs, axis=0, keepdims=True)
    a = outer_dft(tabs["wf_filt"], full)
    return filter_spectrum(tabs["tf"], a, inv_l1)


def hyena_mixer(proj, tabs, kf, short_w, bias):
    seq = proj.shape[0]
    z, x1, x2 = short_conv(proj, short_w)
    for order, gate in enumerate((x1, x2)):
        a = outer_dft(tabs["wf_data"], z)
        g_re, g_im = spectral_conv(tabs["tf"], tabs["ti"], a, kf, order)
        z = outer_inverse_gate(tabs["wi"], g_re, g_im, gate, z,
                               bias[order].reshape(1, HY_W).astype(F32), token_major_out=(order == 1))
    return z.reshape(seq, HY_W)


def _chunk_rows(c, t):
    return pl.ds(pl.multiple_of(c * t, t), t)


def _retention_kernel(lg_ref, q_ref, k_ref, v_ref, g_ref, o_ref, rstore_ref, s_ref, *, nblk, cpb, t):
    h = pl.program_id(0)
    sweep = pl.program_id(1)
    i = pl.program_id(2)
    lgf = lg_ref[0, h]
    lgb = lg_ref[1, h]
    pos = lax.broadcasted_iota(jnp.int32, (t, 1), 0).astype(F32)
    chunk_len = jnp.full((1, RET_DV), float(t), F32)

    @pl.when(i == 0)
    def _():
        s_ref[...] = jnp.zeros_like(s_ref)

    @pl.when(sweep == 0)
    def _():
        def body(c, carry):
            cc = cpb - 1 - c
            rows = _chunk_rows(cc, t)
            rstore_ref[(nblk - 1 - i) * cpb + cc] = s_ref[...]
            kw = (k_ref[rows, :].astype(F32) * jnp.exp(lgb * pos)).astype(BF16)
            s_ref[...] = s_ref[...] * jnp.exp(lgb * chunk_len) + _dot_tn(kw, v_ref[rows, :])
            return carry

        lax.fori_loop(0, cpb, body, 0)

    @pl.when(sweep == 1)
    def _():
        ri = lax.broadcasted_iota(jnp.int32, (t, t), 0)
        ci = lax.broadcasted_iota(jnp.int32, (t, t), 1)
        diff = (ri - ci).astype(F32)
        decay = (jnp.where(diff >= 0, jnp.exp(lgf * jnp.maximum(diff, 0.0)), 0.0)
                 + jnp.where(diff <= 0, jnp.exp(lgb * jnp.maximum(-diff, 0.0)), 0.0))
        q_fwd = jnp.exp(lgf * (pos + 1.0)) * (RET_DK ** -0.5)
        q_bwd = jnp.exp(lgb * (t - pos)) * (RET_DK ** -0.5)
        k_fwd = jnp.exp(lgf * (t - 1.0 - pos))

        def body(c, carry):
            rows = _chunk_rows(c, t)
            q = q_ref[rows, :].astype(F32)
            k = k_ref[rows, :]
            v = v_ref[rows, :]
            scores = _dot_nt((q * (RET_DK ** -0.5)).astype(BF16), k) * decay
            y = _dot(scores.astype(BF16), v)
            y = y + _dot((q * q_fwd).astype(BF16), s_ref[...].astype(BF16))
            y = y + _dot((q * q_bwd).astype(BF16), rstore_ref[i * cpb + c].astype(BF16))
            kw = (k.astype(F32) * k_fwd).astype(BF16)
            s_ref[...] = s_ref[...] * jnp.exp(lgf * chunk_len) + _dot_tn(kw, v)
            y = y * lax.rsqrt(jnp.mean(y * y, axis=-1, keepdims=True) + EPS)
            gt = g_ref[rows, :].astype(F32)
            o_ref[rows, :] = (gt * jax.nn.sigmoid(gt) * y).astype(o_ref.dtype)
            return carry

        lax.fori_loop(0, cpb, body, 0)


CHUNKS_PER_STEP = 4


def retention_mixer(proj, col0, log_decay, t=CHUNK):
    seq = proj.shape[0]
    nc = seq // t
    cpb = math.gcd(CHUNKS_PER_STEP, nc)
    nblk = nc // cpb
    tb = cpb * t
    qb = col0 // RET_DK
    kb = qb + RET_H
    vb = (col0 + 2 * RET_H * RET_DK) // RET_DV
    gb = vb + RET_H

    def rows(sweep, i):
        return sweep * i + (1 - sweep) * (nblk - 1 - i)

    return pl.pallas_call(
        functools.partial(_retention_kernel, nblk=nblk, cpb=cpb, t=t),
        grid=(RET_H, 2, nblk),
        in_specs=[pl.BlockSpec(memory_space=pltpu.SMEM),
                  pl.BlockSpec((tb, RET_DK), lambda h, s, i: (s * i, qb + h)),
                  pl.BlockSpec((tb, RET_DK), lambda h, s, i: (rows(s, i), kb + h)),
                  pl.BlockSpec((tb, RET_DV), lambda h, s, i: (rows(s, i), vb + h)),
                  pl.BlockSpec((tb, RET_DV), lambda h, s, i: (s * i, gb + h))],
        out_specs=pl.BlockSpec((tb, RET_DV), lambda h, s, i: (s * i, h)),
        out_shape=jax.ShapeDtypeStruct((seq, RET_H * RET_DV), BF16),
        scratch_shapes=[pltpu.VMEM((nc, RET_DK, RET_DV), F32), pltpu.VMEM((RET_DK, RET_DV), F32)],
        compiler_params=_cparams("arbitrary", "arbitrary", "arbitrary"),
        name="retention",
    )(log_decay, proj, proj, proj, proj)


def _log_sigmoid(x):
    return jnp.minimum(x, 0.0) - jnp.log(1.0 + jnp.exp(-jnp.abs(x)))


def _mlstm_gates(gc_ref, gr_ref, bias_ref, h, direction, rows):
    bi = bias_ref[direction * 2 * ML_H + h]
    bf = bias_ref[direction * 2 * ML_H + ML_H + h]
    a = 2 * direction
    ig_c = gc_ref[0, rows, a:a + 1] + bi
    lf_c = _log_sigmoid(gc_ref[0, rows, a + 1:a + 2] + bf)
    ig_r = gr_ref[0, a:a + 1, rows] + bi
    lf_r = _log_sigmoid(gr_ref[0, a + 1:a + 2, rows] + bf)
    return ig_c, lf_c, ig_r, lf_r


def _split3(x):
    hi = x.astype(BF16)
    rest = x - hi.astype(F32)
    mid = rest.astype(BF16)
    return hi, mid, (rest - mid.astype(F32)).astype(BF16)


def _running_sums(lf_fwd, lf_bwd, as_rows):
    t = lf_fwd.shape[1] if as_rows else lf_fwd.shape[0]
    ri = lax.broadcasted_iota(jnp.int32, (t, t), 0)
    ci = lax.broadcasted_iota(jnp.int32, (t, t), 1)
    tri = (ri >= ci).astype(BF16)
    if as_rows:
        sel = lax.broadcasted_iota(jnp.int32, (8, t), 0)
        both = jnp.where(sel == 0, lf_fwd, jnp.where(sel == 1, lf_bwd, 0.0))
        left = sum(_dot_nt(p, tri) for p in _split3(both))
        left_f, left_b = left[0:1, :], left[1:2, :]
        total_b = jnp.sum(lf_bwd, axis=1, keepdims=True)
    else:
        sel = lax.broadcasted_iota(jnp.int32, (t, LANES), 1)
        both = jnp.where(sel == 0, lf_fwd, jnp.where(sel == 1, lf_bwd, 0.0))
        left = sum(_dot(tri, p) for p in _split3(both))
        left_f, left_b = left[:, 0:1], left[:, 1:2]
        total_b = jnp.sum(lf_bwd, axis=0, keepdims=True)
    return left_f, total_b - left_b + lf_bwd


def _mlstm_state_step(k, v, ig_c, cum_c, total, c_ref, n_ref, m_ref):
    a = total - cum_c + ig_c
    m_loc = jnp.max(a, axis=0, keepdims=True)
    kw = k * jnp.exp(a - m_loc)
    kv = _dot_tn(kw.astype(BF16), v)
    ksum = jnp.sum(kw, axis=0, keepdims=True)
    m_old = m_ref[0:1, 0:1]
    m_new = jnp.maximum(total + m_old, m_loc)
    sp = jnp.exp(total + m_old - m_new)
    sc = jnp.exp(m_loc - m_new)
    c_ref[...] = sp * c_ref[...] + sc * kv
    n_ref[...] = sp * n_ref[...] + sc * jnp.broadcast_to(ksum, n_ref.shape)
    m_ref[...] = jnp.broadcast_to(m_new, m_ref.shape)


def _mlstm_output(qk, q, v, ig_r, cum_c, cum_r, c_prev, n_prev, m_prev, backward):
    t = q.shape[0]
    ri = lax.broadcasted_iota(jnp.int32, (t, t), 0)
    ci = lax.broadcasted_iota(jnp.int32, (t, t), 1)
    keep = (ri <= ci) if backward else (ri >= ci)
    dlog = jnp.where(keep, cum_c - cum_r + ig_r, -jnp.inf)
    inter = cum_c + m_prev
    m_t = jnp.maximum(inter, jnp.max(dlog, axis=-1, keepdims=True))
    s = qk * jnp.exp(dlog - m_t)
    wi = jnp.exp(inter - m_t)
    num = _dot(s.astype(BF16), v) + wi * _dot(q.astype(BF16), c_prev.astype(BF16))
    den = jnp.sum(s, axis=-1, keepdims=True) + wi * jnp.sum(q * n_prev, axis=-1, keepdims=True)
    return num / jnp.maximum(jnp.abs(den), jnp.exp(-m_t))


def _mlstm_kernel(bias_ref, q_ref, k_ref, v_ref, o_ref, gc_ref, gr_ref, gain_ref, out_ref,
                  cstore_ref, nstore_ref, mstore_ref, c_ref, n_ref, m_ref, *, nblk, cpb, t):
    h = pl.program_id(0)
    sweep = pl.program_id(1)
    i = pl.program_id(2)

    @pl.when(i == 0)
    def _():
        c_ref[...] = jnp.zeros_like(c_ref)
        n_ref[...] = jnp.zeros_like(n_ref)
        m_ref[...] = jnp.zeros_like(m_ref)

    @pl.when(sweep == 0)
    def _():
        def body(c, carry):
            cc = cpb - 1 - c
            rows = _chunk_rows(cc, t)
            n = (nblk - 1 - i) * cpb + cc
            cstore_ref[n] = c_ref[...]
            nstore_ref[n] = n_ref[...]
            mstore_ref[n] = m_ref[...]
            k = k_ref[rows, :].astype(F32) * (ML_DK ** -0.5)
            _, lf_c, _, _ = _mlstm_gates(gc_ref, gr_ref, bias_ref, h, 0, rows)
            ig_c, lb_c, _, _ = _mlstm_gates(gc_ref, gr_ref, bias_ref, h, 1, rows)
            _, cumb_c = _running_sums(lf_c, lb_c, False)
            total = jnp.sum(lb_c, axis=0, keepdims=True)
            _mlstm_state_step(k, v_ref[rows, :], ig_c, cumb_c, total, c_ref, n_ref, m_ref)
            return carry

        lax.fori_loop(0, cpb, body, 0)

    @pl.when(sweep == 1)
    def _():
        def body(c, carry):
            rows = _chunk_rows(c, t)
            n = i * cpb + c
            k = k_ref[rows, :].astype(F32) * (ML_DK ** -0.5)
            v = v_ref[rows, :]
            q = q_ref[rows, :].astype(F32)
            qk = _dot_nt(q_ref[rows, :], k.astype(BF16))
            ig_c, lf_c, ig_r, lf_r = _mlstm_gates(gc_ref, gr_ref, bias_ref, h, 0, rows)
            _, lb_c, igb_r, lb_r = _mlstm_gates(gc_ref, gr_ref, bias_ref, h, 1, rows)
            cum_c, cumb_c = _running_sums(lf_c, lb_c, False)
            cum_r, cumb_r = _running_sums(lf_r, lb_r, True)
            hf = _mlstm_output(qk, q, v, ig_r, cum_c, cum_r, c_ref[...], n_ref[0:1, :], m_ref[0:1, 0:1], False)
            total = jnp.sum(lf_c, axis=0, keepdims=True)
            _mlstm_state_step(k, v, ig_c, cum_c, total, c_ref, n_ref, m_ref)
            hb = _mlstm_output(qk, q, v, igb_r, cumb_c, cumb_r, cstore_ref[n], nstore_ref[n][0:1, :],
                               mstore_ref[n][0:1, 0:1], True)
            y = hf + hb
            y = y * lax.rsqrt(jnp.mean(y * y, axis=-1, keepdims=True) + EPS) * gain_ref[...]
            out_ref[rows, :] = (jax.nn.sigmoid(o_ref[rows, :].astype(F32)) * y).astype(out_ref.dtype)
            return carry

        lax.fori_loop(0, cpb, body, 0)


def mlstm_mixer(proj, gates, col0, gate_bias, norm_gain, t=CHUNK):
    seq = proj.shape[0]
    nc = seq // t
    cpb = math.gcd(CHUNKS_PER_STEP, nc)
    nblk = nc // cpb
    tb = cpb * t
    qb = col0 // ML_DK
    kb = qb + ML_H
    vb = (col0 + 2 * ML_H * ML_DK) // ML_DV
    ob = vb + ML_H
    g = gates[:, :4 * ML_H].reshape(seq, 2, 2, ML_H)
    g = jnp.transpose(g, (3, 0, 1, 2)).reshape(ML_H, seq, 4)
    g_cols = g
    g_rows = jnp.transpose(g, (0, 2, 1))

    def rows(sweep, i):
        return sweep * i + (1 - sweep) * (nblk - 1 - i)

    return pl.pallas_call(
        functools.partial(_mlstm_kernel, nblk=nblk, cpb=cpb, t=t),
        grid=(ML_H, 2, nblk),
        in_specs=[pl.BlockSpec(memory_space=pltpu.SMEM),
                  pl.BlockSpec((tb, ML_DK), lambda h, s, i: (s * i, qb + h)),
                  pl.BlockSpec((tb, ML_DK), lambda h, s, i: (rows(s, i), kb + h)),
                  pl.BlockSpec((tb, ML_DV), lambda h, s, i: (rows(s, i), vb + h)),
                  pl.BlockSpec((tb, ML_DV), lambda h, s, i: (s * i, ob + h)),
                  pl.BlockSpec((1, tb, 4), lambda h, s, i: (h, rows(s, i), 0)),
                  pl.BlockSpec((1, 4, tb), lambda h, s, i: (h, 0, rows(s, i))),
                  pl.BlockSpec((1, ML_DV), lambda h, s, i: (0, h))],
        out_specs=pl.BlockSpec((tb, ML_DV), lambda h, s, i: (s * i, h)),
        out_shape=jax.ShapeDtypeStruct((seq, ML_H * ML_DV), BF16),
        scratch_shapes=[pltpu.VMEM((nc, ML_DK, ML_DV), F32), pltpu.VMEM((nc, 8, ML_DK), F32),
                        pltpu.VMEM((nc, 8, LANES), F32), pltpu.VMEM((ML_DK, ML_DV), F32),
                        pltpu.VMEM((8, ML_DK), F32), pltpu.VMEM((8, LANES), F32)],
        compiler_params=_cparams("arbitrary", "arbitrary", "arbitrary"),
        name="mlstm",
    )(gate_bias, proj, proj, proj, proj, g_cols, g_rows, norm_gain.reshape(1, ML_H * ML_DV))


PERM_ROWS = 256
CLASS_RUN = 16
ATT_SUB_ROWS = 128


def _group_permutation(dilation):
    run = CLASS_RUN * dilation
    new = jnp.arange(PERM_ROWS, dtype=jnp.int32)
    within = new % run
    src = (new // run) * run + (within % CLASS_RUN) * dilation + within // CLASS_RUN
    return (src[:, None] == jnp.arange(PERM_ROWS, dtype=jnp.int32)[None, :]).astype(BF16)


def _attention_prep_kernel(gain_ref, p4_ref, p16_ref, x_ref, o1_ref, o4_ref, o16_ref):
    part = pl.program_id(1)

    @pl.when(part < 2)
    def _():
        gain = gain_ref[0]
        for h in range(ATT_H):
            cols = slice(h * ATT_DH, (h + 1) * ATT_DH)
            x = x_ref[:, cols].astype(F32)
            o1_ref[:, cols] = (x * lax.rsqrt(jnp.mean(x * x, axis=-1, keepdims=True) + EPS) * gain).astype(BF16)

    @pl.when(part == 2)
    def _():
        o1_ref[...] = x_ref[...]

    for s in range(x_ref.shape[0] // PERM_ROWS):
        rows = slice(s * PERM_ROWS, (s + 1) * PERM_ROWS)
        x = o1_ref[rows, :]
        o4_ref[rows, :] = _dot(p4_ref[...], x).astype(BF16)
        o16_ref[rows, :] = _dot(p16_ref[...], x).astype(BF16)


def attention_prep(proj, col0, qk_gain, tr=512):
    seq = proj.shape[0]
    cb = col0 // W_GROUP
    gains = jnp.stack([qk_gain[0] * (ATT_DH ** -0.5), qk_gain[1], jnp.ones_like(qk_gain[0])]).reshape(3, 1, ATT_DH)
    out = jax.ShapeDtypeStruct((seq, 3 * W_GROUP), BF16)
    ospec = pl.BlockSpec((tr, W_GROUP), lambda i, j: (i, j))
    pspec = pl.BlockSpec((PERM_ROWS, PERM_ROWS), lambda i, j: (0, 0))
    return pl.pallas_call(
        _attention_prep_kernel,
        grid=(seq // tr, 3),
        in_specs=[pl.BlockSpec((1, 1, ATT_DH), lambda i, j: (j, 0, 0)), pspec, pspec,
                  pl.BlockSpec((tr, W_GROUP), lambda i, j: (i, cb + j))],
        out_specs=[ospec, ospec, ospec],
        out_shape=[out, out, out],
        compiler_params=_cparams("parallel", "arbitrary"),
        name="attention_prep",
    )(gains.astype(F32), _group_permutation(4), _group_permutation(16), proj)


def _band_attention_kernel(q_ref, kp_ref, kc_ref, kn_ref, vp_ref, vc_ref, vn_ref, o_ref, lse_ref,
                           *, dilation, nblk):
    i = pl.program_id(1)
    tq = q_ref.shape[0] * CLASS_RUN
    hs = ATT_HALF_STEPS
    sq = min(ATT_SUB_ROWS, tq)
    sk = sq + 2 * hs
    ri = lax.broadcasted_iota(jnp.int32, (sq, sk), 0)
    ci = lax.broadcasted_iota(jnp.int32, (sq, sk), 1)
    off = ci - hs - ri
    first_col = jnp.where(i > 0, 0, hs)
    end_col = jnp.where(i < nblk - 1, tq + 2 * hs, tq + hs)
    in_band = jnp.abs(off) <= hs
    dist = (jnp.abs(off) * dilation).astype(F32)
    valid = [in_band & (ci + s0 >= first_col) & (ci + s0 < end_col) for s0 in range(0, tq, sq)]
    lane = lax.broadcasted_iota(jnp.int32, (sq, LANES), 1)
    lse_all = [jnp.zeros((sq, LANES), F32) for _ in valid]

    def rows(ref, cols):
        x = ref[:, :, cols]
        return x.reshape(x.shape[0] * CLASS_RUN, x.shape[2])

    for h in range(ATT_H):
        cols = slice(h * ATT_DH, (h + 1) * ATT_DH)
        slope = 2.0 ** (-8.0 * (h + 1) / ATT_H)
        qq = rows(q_ref, cols)
        kk = jnp.concatenate([rows(kp_ref, cols), rows(kc_ref, cols), rows(kn_ref, cols)], axis=0)
        vv = jnp.concatenate([rows(vp_ref, cols), rows(vc_ref, cols), rows(vn_ref, cols)], axis=0)
        outs = []
        for b, s0 in enumerate(range(0, tq, sq)):
            s = _dot_nt(qq[s0:s0 + sq], kk[s0:s0 + sk]) - slope * dist
            s = jnp.where(valid[b], s, NEG)
            m = jnp.max(s, axis=-1, keepdims=True)
            p = jnp.exp(s - m)
            den = jnp.sum(p, axis=-1, keepdims=True)
            outs.append(_dot(p.astype(BF16), vv[s0:s0 + sk]) / den)
            lse_all[b] = jnp.where(lane == h, m + jnp.log(den), lse_all[b])
        o = jnp.concatenate(outs, axis=0)
        o_ref[:, :, cols] = o.astype(o_ref.dtype).reshape(tq // CLASS_RUN, CLASS_RUN, ATT_DH)
    lse_ref[...] = jnp.concatenate(lse_all, axis=0).reshape(tq // CLASS_RUN, CLASS_RUN, LANES)


def band_attention(qkv, dilation, tq=256):
    seq = qkv.shape[0]
    n = seq // dilation
    tq = min(tq, n)
    nblk = n // tq
    hs = ATT_HALF_STEPS
    runs = n // CLASS_RUN
    tr = tq // CLASS_RUN
    hr = hs // CLASS_RUN
    ratio = tq // hs
    last_halo = n // hs - 1
    view = qkv.reshape(runs, dilation, CLASS_RUN, 3 * W_GROUP)

    def cur(part):
        return pl.BlockSpec((tr, None, CLASS_RUN, W_GROUP), lambda r, i: (i, r, 0, part))

    def prev(part):
        return pl.BlockSpec((hr, None, CLASS_RUN, W_GROUP), lambda r, i: (jnp.maximum(i * ratio - 1, 0), r, 0, part))

    def nxt(part):
        return pl.BlockSpec((hr, None, CLASS_RUN, W_GROUP),
                            lambda r, i: (jnp.minimum((i + 1) * ratio, last_halo), r, 0, part))

    o, lse = pl.pallas_call(
        functools.partial(_band_attention_kernel, dilation=dilation, nblk=nblk),
        grid=(dilation, nblk),
        in_specs=[cur(0), prev(1), cur(1), nxt(1), prev(2), cur(2), nxt(2)],
        out_specs=[pl.BlockSpec((tr, None, CLASS_RUN, W_GROUP), lambda r, i: (i, r, 0, 0)),
                   pl.BlockSpec((tr, None, CLASS_RUN, LANES), lambda r, i: (i, r, 0, 0))],
        out_shape=[jax.ShapeDtypeStruct((runs, dilation, CLASS_RUN, W_GROUP), BF16),
                   jax.ShapeDtypeStruct((runs, dilation, CLASS_RUN, LANES), F32)],
        compiler_params=_cparams("parallel", "arbitrary"),
        name=f"band_attention_d{dilation}",
    )(view, view, view, view, view, view, view)
    return o.reshape(seq, W_GROUP), lse.reshape(seq, LANES)


def _merge_branches_kernel(q4_ref, q16_ref, o1_ref, o4_ref, o16_ref, l1_ref, l4_ref, l16_ref, out_ref):
    def ungroup(qt, x):
        return _dot(qt, x)

    def ungroup_f32(qt, x):
        hi = x.astype(BF16)
        lo = (x - hi.astype(F32)).astype(BF16)
        return _dot(qt, hi) + _dot(qt, lo)

    for s in range(o1_ref.shape[0] // PERM_ROWS):
        rows = slice(s * PERM_ROWS, (s + 1) * PERM_ROWS)
        q4, q16 = q4_ref[...], q16_ref[...]
        l1 = l1_ref[rows, :]
        l2 = ungroup_f32(q4, l4_ref[rows, :])
        l3 = ungroup_f32(q16, l16_ref[rows, :])
        m = jnp.maximum(jnp.maximum(l1, l2), l3)
        e1, e2, e3 = jnp.exp(l1 - m), jnp.exp(l2 - m), jnp.exp(l3 - m)
        inv = 1.0 / (e1 + e2 + e3)
        w1, w2, w3 = e1 * inv, e2 * inv, e3 * inv
        o2 = ungroup(q4, o4_ref[rows, :])
        o3 = ungroup(q16, o16_ref[rows, :])
        for h in range(ATT_H):
            cols = slice(h * ATT_DH, (h + 1) * ATT_DH)
            out_ref[rows, cols] = (w1[:, h:h + 1] * o1_ref[rows, cols].astype(F32)
                                   + w2[:, h:h + 1] * o2[:, cols]
                                   + w3[:, h:h + 1] * o3[:, cols]).astype(out_ref.dtype)


def merge_branches(outs, lses, tr=512):
    seq = outs[0].shape[0]
    ospec = pl.BlockSpec((tr, W_GROUP), lambda i: (i, 0))
    lspec = pl.BlockSpec((tr, LANES), lambda i: (i, 0))
    pspec = pl.BlockSpec((PERM_ROWS, PERM_ROWS), lambda i: (0, 0))
    return pl.pallas_call(
        _merge_branches_kernel,
        grid=(seq // tr,),
        in_specs=[pspec, pspec, ospec, ospec, ospec, lspec, lspec, lspec],
        out_specs=ospec,
        out_shape=jax.ShapeDtypeStruct((seq, W_GROUP), BF16),
        compiler_params=_cparams("parallel"),
        name="attention_merge",
    )(_group_permutation(4).T, _group_permutation(16).T, *outs, *lses)


def dilated_attention(proj, col0, qk_gain):
    grouped = attention_prep(proj, col0, qk_gain)
    outs, lses = [], []
    for d, qkv in zip(ATT_DILATIONS, grouped):
        o, lse = band_attention(qkv, d)
        outs.append(o)
        lses.append(lse)
    return merge_branches(outs, lses)


def kernel(x, c, ada_w, ada_b, ada_table, w_in, w_out, hy_short, hy_w1, hy_b1, hy_w2, hy_b2, hy_w3, hy_bias,
           ret_decay, att_qk_gain, ml_gate_bias, ml_norm_gain, ffn_w1, ffn_w3, ffn_w2):
    batch, seq, d_model = x.shape
    depth = w_in.shape[0]
    d_main = 12 * W_GROUP
    hidden = ffn_w1.shape[2]
    hidden_pad = -(-hidden // 1024) * 1024

    tabs = _dft_tables(seq)
    consts = _filter_positions(seq)
    mod_shared = ada_modulation(c, ada_w, ada_b)

    pad = hidden_pad - hidden
    w_in_b = w_in.astype(BF16)
    w_gate_b = jnp.pad(w_in[:, :, d_main:], ((0, 0), (0, 0), (0, LANES - (w_in.shape[2] - d_main)))).astype(BF16)
    w_out_b = w_out.astype(BF16)
    w1_b = jnp.pad(ffn_w1, ((0, 0), (0, 0), (0, pad))).astype(BF16)
    w3_b = jnp.pad(ffn_w3, ((0, 0), (0, 0), (0, pad))).astype(BF16)
    w2_b = jnp.pad(ffn_w2, ((0, 0), (0, pad), (0, 0))).astype(BF16)

    rows = x.reshape(batch * seq, d_model)
    outs = []
    for b in range(batch):
        xb = rows[b * seq:(b + 1) * seq]
        for l in range(depth):
            mod = (mod_shared[b:b + 1] + ada_table[l].reshape(1, -1)).reshape(6, d_model)
            sh1, sc1, g1, sh2, sc2, g2 = (mod[i:i + 1] for i in range(6))
            proj, gates = norm_proj(xb, 1.0 + sc1, sh1, w_in_b, l, d_main, w_gate_b[l])

            kf = hyena_filter_spectra(seq, tabs, consts, hy_w1[l], hy_b1[l], hy_w2[l], hy_b2[l], hy_w3[l])
            y_a = hyena_mixer(proj, tabs, kf, hy_short[l], hy_bias[l])
            y_b = retention_mixer(proj, 3 * W_GROUP, jax.nn.log_sigmoid(ret_decay[l].astype(F32)))
            y_c = dilated_attention(proj, 6 * W_GROUP, att_qk_gain[l])
            y_d = mlstm_mixer(proj, gates, 9 * W_GROUP, ml_gate_bias[l], ml_norm_gain[l])
            y = jnp.concatenate([y_a.astype(BF16), y_b, y_c, y_d], axis=-1)
            xb = mm_residual(y, w_out_b, l, xb, g1, tm=1024, tn=512, tk=y.shape[1])

            u = norm_swiglu(xb, 1.0 + sc2, sh2, w1_b, w3_b, l)
            xb = mm_residual(u, w2_b, l, xb, g2, tm=1024, tn=1024, tk=hidden_pad // 4)
        outs.append(xb)
    return jnp.concatenate(outs, 0).reshape(batch, seq, d_model)
```

```python
import functools
import math

import numpy as np
import jax
import jax.numpy as jnp
from jax import lax
from jax.experimental import pallas as pl
from jax.experimental.pallas import tpu as pltpu

F32 = jnp.float32
BF16 = jnp.bfloat16
HIGHEST = lax.Precision.HIGHEST

EPS = 1e-6
NEG = -1e30

V7X_VMEM_LIMIT_BYTES = 56 * 1024 * 1024
LANES = 128

W_GROUP = 1024
RET_H, RET_DK, RET_DV = 4, 128, 256
ATT_H, ATT_DH = 8, 128
ATT_HALF_STEPS = 64
ATT_DILATIONS = (1, 4, 16)
ML_H, ML_DK, ML_DV = 4, 128, 256
HY_BANDS = 16
HY_W = W_GROUP
HY_FILTER_WIDTH = 64
HY_TARGET, HY_SHORT_PCT, HY_LONG_PCT = 1e-2, 0.3, 1.5
HY_FEAT_PAD = 40
DFT_N2 = 128
CHUNK = 256


def _cparams(*sem):
    return pltpu.CompilerParams(dimension_semantics=sem, vmem_limit_bytes=V7X_VMEM_LIMIT_BYTES)


def _dot(a, b):
    return jnp.dot(a, b, preferred_element_type=F32)


def _dot_nt(a, b):
    return lax.dot_general(a, b, (((1,), (1,)), ((), ())), preferred_element_type=F32)


def _dot_tn(a, b):
    return lax.dot_general(a, b, (((0,), (0,)), ((), ())), preferred_element_type=F32)


def _ada_kernel(c_ref, w_ref, b_ref, o_ref):
    c = c_ref[...]
    s = c * jax.nn.sigmoid(c)
    o_ref[...] = jnp.dot(s, w_ref[...], preferred_element_type=F32, precision=HIGHEST) + b_ref[...]


def ada_modulation(c, ada_w, ada_b):
    d, n = ada_w.shape
    tn = 512
    c8 = jnp.broadcast_to(c.reshape(1, d), (8, d))
    out = pl.pallas_call(
        _ada_kernel,
        grid=(n // tn,),
        in_specs=[pl.BlockSpec((8, d), lambda j: (0, 0)),
                  pl.BlockSpec((d, tn), lambda j: (0, j)),
                  pl.BlockSpec((1, tn), lambda j: (0, j))],
        out_specs=pl.BlockSpec((8, tn), lambda j: (0, j)),
        out_shape=jax.ShapeDtypeStruct((8, n), F32),
        compiler_params=_cparams("parallel"),
        name="ada_modulation",
    )(c8, ada_w, ada_b.reshape(1, n))
    return out[0:1]


NORM_ROWS = 64


def _normalise_into(x_ref, sc_ref, sh_ref, h_ref):
    tm = x_ref.shape[0]
    sc = sc_ref[...]
    sh = sh_ref[...]

    def body(r, carry):
        rows = pl.ds(pl.multiple_of(r * NORM_ROWS, NORM_ROWS), NORM_ROWS)
        x = x_ref[rows, :]
        ms = jnp.mean(x * x, axis=-1, keepdims=True)
        h_ref[rows, :] = (x * lax.rsqrt(ms + EPS) * sc + sh).astype(h_ref.dtype)
        return carry

    lax.fori_loop(0, tm // NORM_ROWS, body, 0)


def _norm_proj_kernel(x_ref, sc_ref, sh_ref, w_ref, wg_ref, o_ref, g_ref, h_ref):
    @pl.when(pl.program_id(1) == 0)
    def _():
        _normalise_into(x_ref, sc_ref, sh_ref, h_ref)
        g_ref[...] = _dot(h_ref[...], wg_ref[...])

    o_ref[...] = _dot(h_ref[...], w_ref[...]).astype(o_ref.dtype)


def norm_proj(x, scale1p, shift, w_stack, layer, n, wg, tm=512, tn=1024):
    m, d = x.shape
    ng = wg.shape[1]
    return pl.pallas_call(
        _norm_proj_kernel,
        grid=(m // tm, n // tn),
        in_specs=[pl.BlockSpec((tm, d), lambda i, j: (i, 0)),
                  pl.BlockSpec((1, d), lambda i, j: (0, 0)),
                  pl.BlockSpec((1, d), lambda i, j: (0, 0)),
                  pl.BlockSpec((None, d, tn), lambda i, j: (layer, 0, j)),
                  pl.BlockSpec((d, ng), lambda i, j: (0, 0))],
        out_specs=[pl.BlockSpec((tm, tn), lambda i, j: (i, j)),
                   pl.BlockSpec((tm, ng), lambda i, j: (i, 0))],
        out_shape=[jax.ShapeDtypeStruct((m, n), BF16), jax.ShapeDtypeStruct((m, ng), F32)],
        scratch_shapes=[pltpu.VMEM((tm, d), BF16)],
        compiler_params=_cparams("parallel", "arbitrary"),
        name="norm_proj",
    )(x, scale1p, shift, w_stack, wg)


def _norm_swiglu_kernel(x_ref, sc_ref, sh_ref, w1_ref, w3_ref, o_ref, h_ref):
    @pl.when(pl.program_id(1) == 0)
    def _():
        _normalise_into(x_ref, sc_ref, sh_ref, h_ref)

    h = h_ref[...]
    a = _dot(h, w1_ref[...])
    b = _dot(h, w3_ref[...])
    o_ref[...] = (a * jax.nn.sigmoid(a) * b).astype(o_ref.dtype)


def norm_swiglu(x, scale1p, shift, w1_stack, w3_stack, layer, tm=1024, tn=256):
    m, d = x.shape
    n = w1_stack.shape[2]
    return pl.pallas_call(
        _norm_swiglu_kernel,
        grid=(m // tm, n // tn),
        in_specs=[pl.BlockSpec((tm, d), lambda i, j: (i, 0)),
                  pl.BlockSpec((1, d), lambda i, j: (0, 0)),
                  pl.BlockSpec((1, d), lambda i, j: (0, 0)),
                  pl.BlockSpec((None, d, tn), lambda i, j: (layer, 0, j)),
                  pl.BlockSpec((None, d, tn), lambda i, j: (layer, 0, j))],
        out_specs=pl.BlockSpec((tm, tn), lambda i, j: (i, j)),
        out_shape=jax.ShapeDtypeStruct((m, n), BF16),
        scratch_shapes=[pltpu.VMEM((tm, d), BF16)],
        compiler_params=_cparams("parallel", "arbitrary"),
        name="norm_swiglu",
    )(x, scale1p, shift, w1_stack, w3_stack)


def _mm_residual_kernel(y_ref, w_ref, x_ref, g_ref, o_ref, acc_ref, *, nk):
    k = pl.program_id(2)
    part = _dot(y_ref[...], w_ref[...])

    @pl.when(k == 0)
    def _():
        acc_ref[...] = part

    @pl.when(k > 0)
    def _():
        acc_ref[...] += part

    @pl.when(k == nk - 1)
    def _():
        o_ref[...] = x_ref[...] + g_ref[...] * acc_ref[...]


def _mm_residual_1k_kernel(y_ref, w_ref, x_ref, g_ref, o_ref):
    o_ref[...] = x_ref[...] + g_ref[...] * _dot(y_ref[...], w_ref[...])


def mm_residual(y, w_stack, layer, x, gate, tm, tn, tk):
    m, kk = y.shape
    n = w_stack.shape[2]
    nk = kk // tk
    if nk == 1:
        return pl.pallas_call(
            _mm_residual_1k_kernel,
            grid=(m // tm, n // tn),
            in_specs=[pl.BlockSpec((tm, kk), lambda i, j: (i, 0)),
                      pl.BlockSpec((None, kk, tn), lambda i, j: (layer, 0, j)),
                      pl.BlockSpec((tm, tn), lambda i, j: (i, j)),
                      pl.BlockSpec((1, tn), lambda i, j: (0, j))],
            out_specs=pl.BlockSpec((tm, tn), lambda i, j: (i, j)),
            out_shape=jax.ShapeDtypeStruct((m, n), F32),
            compiler_params=_cparams("parallel", "arbitrary"),
            name="mm_residual",
        )(y, w_stack, x, gate)
    return pl.pallas_call(
        functools.partial(_mm_residual_kernel, nk=nk),
        grid=(m // tm, n // tn, nk),
        in_specs=[pl.BlockSpec((tm, tk), lambda i, j, k: (i, k)),
                  pl.BlockSpec((None, tk, tn), lambda i, j, k: (layer, k, j)),
                  pl.BlockSpec((tm, tn), lambda i, j, k: (i, j)),
                  pl.BlockSpec((1, tn), lambda i, j, k: (0, j))],
        out_specs=pl.BlockSpec((tm, tn), lambda i, j, k: (i, j)),
        out_shape=jax.ShapeDtypeStruct((m, n), F32),
        scratch_shapes=[pltpu.VMEM((tm, tn), F32)],
        compiler_params=_cparams("parallel", "arbitrary", "arbitrary"),
        name="mm_residual_ksplit",
    )(y, w_stack, x, gate)


def _dft_tables(seq_len):
    n = 2 * seq_len
    n2 = DFT_N2
    n1 = n // n2
    half = n1 // 2
    kb = half + 1
    kbp = -(-kb // 8) * 8
    k1 = jnp.arange(kbp, dtype=jnp.int32)
    live = (k1 <= half)
    col = jnp.arange(n1, dtype=jnp.int32)
    ph = (2.0 * math.pi / n1) * ((k1[:, None] * col[None, :]) % n1).astype(F32)
    wf = jnp.stack([jnp.cos(ph), -jnp.sin(ph)], axis=0) * live[None, :, None]
    wf = wf.reshape(2 * kbp, n1)
    a = jnp.arange(n2, dtype=jnp.int32)
    mm = (a[None, None, :] * (k1[:, None, None] + n1 * a[None, :, None])) % n
    th = (2.0 * math.pi / n) * mm.astype(F32)
    c, s = jnp.cos(th), jnp.sin(th)
    tf = jnp.concatenate([jnp.concatenate([c, s], 2), jnp.concatenate([-s, c], 2)], 1)
    tf = tf * live[:, None, None]
    ti = jnp.transpose(tf, (0, 2, 1))
    wt = jnp.where((k1 == 0) | (k1 == half), 1.0, 2.0) * live / n
    row = jnp.arange(half, dtype=jnp.int32)
    ph2 = (2.0 * math.pi / n1) * ((row[:, None] * k1[None, :]) % n1).astype(F32)
    return dict(wf_data=wf[:, :half].astype(BF16), wf_filt=wf.astype(BF16), tf=tf.astype(BF16),
                ti=ti.astype(BF16), wi_re=(wt * jnp.cos(ph2)).astype(BF16), wi_im=(-wt * jnp.sin(ph2)).astype(BF16),
                kbp=kbp, n1=n1)


INNER_STEP = 8


U32 = jnp.uint32


def _pack_pair(hi, lo):
    hb = lax.bitcast_convert_type(hi.astype(BF16).astype(F32), U32)
    lb = lax.bitcast_convert_type(lo.astype(BF16).astype(F32), U32)
    return hb | (lb >> 16)


def _unpack_pair(w):
    hi = lax.bitcast_convert_type(w & jnp.uint32(0xFFFF0000), F32).astype(BF16)
    lo = lax.bitcast_convert_type(w << 16, F32).astype(BF16)
    return hi, lo


def _outer_dft_kernel(w_ref, x_ref, o_ref, *, packed_in):
    kbp = o_ref.shape[0]
    w = w_ref[...]
    for j in range(x_ref.shape[0]):
        if packed_in:
            parts = _unpack_pair(x_ref[j])
        else:
            parts = (x_ref[j].astype(BF16),)
        tc = parts[0].shape[1]
        for p, x in enumerate(parts):
            a = _dot(w, x)
            o_ref[:, j, p * tc:(p + 1) * tc] = _pack_pair(a[:kbp], a[kbp:])


def outer_dft(w, x, tc=1024):
    r2, k = w.shape
    inner, _, ch = x.shape
    packed_in = x.dtype == U32
    mult = 2 if packed_in else 1
    return pl.pallas_call(
        functools.partial(_outer_dft_kernel, packed_in=packed_in),
        grid=(inner // INNER_STEP, ch // tc),
        in_specs=[pl.BlockSpec((r2, k), lambda i, j: (0, 0)),
                  pl.BlockSpec((INNER_STEP, k, tc), lambda i, j: (i, 0, j))],
        out_specs=pl.BlockSpec((r2 // 2, INNER_STEP, mult * tc), lambda i, j: (0, i, j)),
        out_shape=jax.ShapeDtypeStruct((r2 // 2, inner, mult * ch), U32),
        compiler_params=_cparams("parallel", "parallel"),
        name="hyena_outer_dft",
    )(w, x)


def _filter_spectrum_kernel(tf_ref, a_ref, inv_ref, o_ref):
    inv = inv_ref[...]
    for b in range(tf_ref.shape[0]):
        x = jnp.concatenate(_unpack_pair(a_ref[b]), axis=0)
        o_ref[b] = (_dot(tf_ref[b], x) * inv).astype(o_ref.dtype)


def filter_spectrum(tf, a, inv_l1, kb=4, tc=1024):
    kbp, inner, ch = a.shape
    r = 2 * inner
    return pl.pallas_call(
        _filter_spectrum_kernel,
        grid=(kbp // kb, ch // tc),
        in_specs=[pl.BlockSpec((kb, r, r), lambda i, j: (i, 0, 0)),
                  pl.BlockSpec((kb, inner, tc), lambda i, j: (i, 0, j)),
                  pl.BlockSpec((1, tc), lambda i, j: (0, j))],
        out_specs=pl.BlockSpec((kb, r, tc), lambda i, j: (i, 0, j)),
        out_shape=jax.ShapeDtypeStruct((kbp, r, ch), BF16),
        compiler_params=_cparams("parallel", "arbitrary"),
        name="hyena_filter_spectrum",
    )(tf, a, inv_l1)


def _spectral_conv_kernel(tf_ref, ti_ref, a_ref, k_ref, g_ref):
    half = a_ref.shape[1]
    for b in range(tf_ref.shape[0]):
        x = jnp.concatenate(_unpack_pair(a_ref[b]), axis=0)
        z = _dot(tf_ref[b], x)
        zr, zi = z[:half], z[half:]
        kr = k_ref[b, :half, :].astype(F32)
        ki = k_ref[b, half:, :].astype(F32)
        y = jnp.concatenate([zr * kr - zi * ki, zr * ki + zi * kr], axis=0).astype(BF16)
        g = _dot(ti_ref[b], y)
        g_ref[:, b, :] = _pack_pair(g[:half], g[half:])


def spectral_conv(tf, ti, a, kf, order, tc=512):
    kbp, inner, ch = a.shape
    r = 2 * inner
    kb = INNER_STEP
    ncb = ch // tc
    return pl.pallas_call(
        _spectral_conv_kernel,
        grid=(kbp // kb, ncb),
        in_specs=[pl.BlockSpec((kb, r, r), lambda i, j: (i, 0, 0)),
                  pl.BlockSpec((kb, r, r), lambda i, j: (i, 0, 0)),
                  pl.BlockSpec((kb, inner, tc), lambda i, j: (i, 0, j)),
                  pl.BlockSpec((kb, r, tc), lambda i, j: (i, 0, order * ncb + j))],
        out_specs=pl.BlockSpec((inner, kb, tc), lambda i, j: (0, i, j)),
        out_shape=jax.ShapeDtypeStruct((inner, kbp, ch), U32),
        compiler_params=_cparams("parallel", "arbitrary"),
        name="hyena_spectral_conv",
    )(tf, ti, a, kf)


def _outer_inverse_gate_kernel(wre_ref, wim_ref, g_ref, gate_ref, zp_ref, bias_ref, o_ref, *, token_major_out):
    wre = wre_ref[...]
    wim = wim_ref[...]
    bias = bias_ref[...]
    for j in range(g_ref.shape[0]):
        g_re, g_im = _unpack_pair(g_ref[j])
        z = gate_ref[j] * (_dot(wre, g_re) + _dot(wim, g_im) + bias * zp_ref[j])
        if token_major_out:
            o_ref[:, j, :] = z
        else:
            o_ref[j] = z


def outer_inverse_gate(wi_re, wi_im, g, gate, z_prev, bias, token_major_out, tc=512):
    r, kbp = wi_re.shape
    inner, _, ch = g.shape
    wspec = pl.BlockSpec((r, kbp), lambda i, j: (0, 0))
    zspec = pl.BlockSpec((INNER_STEP, r, tc), lambda i, j: (i, 0, j))
    if token_major_out:
        ospec, oshape = pl.BlockSpec((r, INNER_STEP, tc), lambda i, j: (0, i, j)), (r, inner, ch)
    else:
        ospec, oshape = zspec, (inner, r, ch)
    return pl.pallas_call(
        functools.partial(_outer_inverse_gate_kernel, token_major_out=token_major_out),
        grid=(inner // INNER_STEP, ch // tc),
        in_specs=[wspec, wspec, pl.BlockSpec((INNER_STEP, kbp, tc), lambda i, j: (i, 0, j)), zspec, zspec,
                  pl.BlockSpec((1, tc), lambda i, j: (0, j))],
        out_specs=ospec,
        out_shape=jax.ShapeDtypeStruct(oshape, F32),
        compiler_params=_cparams("parallel", "parallel"),
        name="hyena_outer_inverse_gate",
    )(wi_re, wi_im, g, gate, z_prev, bias)


def _short_conv_kernel(u_ref, prev_ref, next_ref, w_ref, v_ref, x1_ref, x2_ref):
    tr = u_ref.shape[0]
    ch = v_ref.shape[2]
    row = lax.broadcasted_iota(jnp.int32, (tr, ch), 0)
    for part, o_ref in enumerate((v_ref, x1_ref, x2_ref)):
        cols = slice(part * ch, (part + 1) * ch)
        x = u_ref[:, cols].astype(F32)
        before = jnp.where(row == 0, prev_ref[0, :, cols].astype(F32), pltpu.roll(x, 1, axis=0))
        after = jnp.where(row == tr - 1, next_ref[0, :, cols].astype(F32), pltpu.roll(x, tr - 1, axis=0))
        w = w_ref[:, cols]
        y = before * w[0:1] + x * w[1:2] + after * w[2:3]
        for j in range(tr // DFT_N2):
            o_ref[:, j, :] = y[j * DFT_N2:(j + 1) * DFT_N2]


def short_conv(proj, short_w):
    seq = proj.shape[0]
    width = 3 * HY_W
    tr = INNER_STEP * DFT_N2
    nb = seq // tr
    zero = jnp.zeros((1, width), proj.dtype)
    prev_rows = jnp.concatenate([zero, proj[tr - 1::tr, :width][:-1]], 0).reshape(nb, 1, width)
    next_rows = jnp.concatenate([proj[tr::tr, :width], zero], 0).reshape(nb, 1, width)
    out = jax.ShapeDtypeStruct((DFT_N2, seq // DFT_N2, HY_W), F32)
    return pl.pallas_call(
        _short_conv_kernel,
        grid=(nb,),
        in_specs=[pl.BlockSpec((tr, width), lambda i: (i, 0)),
                  pl.BlockSpec((1, 1, width), lambda i: (i, 0, 0)),
                  pl.BlockSpec((1, 1, width), lambda i: (i, 0, 0)),
                  pl.BlockSpec((3, width), lambda i: (0, 0))],
        out_specs=[pl.BlockSpec((DFT_N2, INNER_STEP, HY_W), lambda i: (0, i, 0))] * 3,
        out_shape=[out, out, out],
        compiler_params=_cparams("parallel"),
        name="hyena_short_conv",
    )(proj, prev_rows, next_rows, short_w)


def _filter_mlp_kernel(z_ref, w1_ref, b1_ref, w2_ref, b2_ref, o_ref):
    hdot = functools.partial(jnp.dot, preferred_element_type=F32, precision=HIGHEST)
    h = jnp.sin(hdot(z_ref[...], w1_ref[...]) + b1_ref[...])
    for i in range(w2_ref.shape[0]):
        h = jnp.sin(hdot(h, w2_ref[i]) + b2_ref[i])
    o_ref[...] = h


def filter_mlp(feat, w1p, b1, w2, b2, tr=1024):
    seq, fp = feat.shape
    wd = w1p.shape[1]
    ni = w2.shape[0]
    tr = min(tr, seq)
    return pl.pallas_call(
        _filter_mlp_kernel,
        grid=(seq // tr,),
        in_specs=[pl.BlockSpec((tr, fp), lambda i: (i, 0)),
                  pl.BlockSpec((fp, wd), lambda i: (0, 0)),
                  pl.BlockSpec((1, wd), lambda i: (0, 0)),
                  pl.BlockSpec((ni, wd, wd), lambda i: (0, 0, 0)),
                  pl.BlockSpec((ni, 1, wd), lambda i: (0, 0, 0))],
        out_specs=pl.BlockSpec((tr, wd), lambda i: (i, 0)),
        out_shape=jax.ShapeDtypeStruct((seq, wd), F32),
        compiler_params=_cparams("parallel"),
        name="hyena_filter_mlp",
    )(feat, w1p, b1.reshape(1, wd), w2, b2.reshape(ni, 1, wd))


def _filter_expand_kernel(h_ref, t_ref, w3_ref, dl_ref, o_ref, s_ref):
    @pl.when(pl.program_id(0) == 0)
    def _():
        s_ref[...] = jnp.zeros_like(s_ref)

    f = _dot(h_ref[...].astype(BF16), w3_ref[0].astype(BF16))
    f = f * jnp.exp(-t_ref[...] * dl_ref[...])
    tr, ch = f.shape
    words = _pack_pair(f[:, :ch // 2], f[:, ch // 2:])
    for j in range(tr // DFT_N2):
        o_ref[:, j, :] = words[j * DFT_N2:(j + 1) * DFT_N2]
    s_ref[...] += jnp.sum(jnp.abs(f).reshape(tr // 8, 8, ch), axis=0)


def filter_expand(h_full, t_full, w3_halves, deltas2):
    n, wd = h_full.shape
    ch = w3_halves.shape[2]
    tr = INNER_STEP * DFT_N2
    nb = n // tr
    return pl.pallas_call(
        _filter_expand_kernel,
        grid=(nb,),
        in_specs=[pl.BlockSpec((tr, wd), lambda i: (i, 0)),
                  pl.BlockSpec((tr, 1), lambda i: (i, 0)),
                  pl.BlockSpec((1, wd, ch), lambda i: (i // (nb // 2), 0, 0)),
                  pl.BlockSpec((1, ch), lambda i: (0, 0))],
        out_specs=[pl.BlockSpec((DFT_N2, INNER_STEP, ch // 2), lambda i: (0, i, 0)),
                   pl.BlockSpec((8, ch), lambda i: (0, 0))],
        out_shape=[jax.ShapeDtypeStruct((DFT_N2, n // DFT_N2, ch // 2), U32), jax.ShapeDtypeStruct((8, ch), F32)],
        compiler_params=_cparams("arbitrary"),
        name="hyena_filter_expand",
    )(h_full, t_full, w3_halves, deltas2)


def _filter_positions(seq_len):
    pos = np.arange(seq_len, dtype=np.float64)
    t = np.linspace(0.0, 1.0, seq_len)
    ang = (2.0 * math.pi / seq_len) * pos
    freqs = np.linspace(1e-4, HY_BANDS - 1, HY_BANDS)
    feat = np.concatenate([t[:, None], np.cos(ang[:, None] * freqs), -np.sin(ang[:, None] * freqs)], -1)
    feat = np.pad(feat, ((0, 0), (0, HY_FEAT_PAD - feat.shape[1])))
    t_full = np.concatenate([t, np.zeros(1), t[:0:-1]])[:, None]
    max_decay = math.log(HY_TARGET) / HY_SHORT_PCT
    min_decay = math.log(HY_TARGET) / HY_LONG_PCT
    deltas = np.abs(np.linspace(min_decay, max_decay, HY_W))
    return (jnp.asarray(feat, F32), jnp.asarray(t_full, F32),
            jnp.asarray(np.tile(deltas, 2)[None, :], F32))


def hyena_filter_spectra(seq_len, tabs, consts, w1, b1, w2, b2, w3):
    feat, t_full, deltas2 = consts
    w1p = jnp.pad(w1, ((0, HY_FEAT_PAD - w1.shape[0]), (0, 0)))
    h = filter_mlp(feat, w1p, b1, w2, b2)
    h_full = jnp.concatenate([h, jnp.zeros((1, h.shape[1]), F32), jnp.flip(h[1:], 0)], 0)
    w3r = w3.reshape(w3.shape[0], 2, 2, HY_W)
    w3_halves = jnp.transpose(w3r, (2, 0, 1, 3)).reshape(2, w3.shape[0], 2 * HY_W)
    full, sabs = filter_expand(h_full, t_full, w3_halves, deltas2)
    inv_l1 = 1.0 / jnp.sum(sabs, axis=0, keepdims=True)
    a = outer_dft(tabs["wf_filt"], full)
    return filter_spectrum(tabs["tf"], a, inv_l1)


def hyena_mixer(proj, tabs, kf, short_w, bias):
    seq = proj.shape[0]
    z, x1, x2 = short_conv(proj, short_w)
    for order, gate in enumerate((x1, x2)):
        a = outer_dft(tabs["wf_data"], z)
        g = spectral_conv(tabs["tf"], tabs["ti"], a, kf, order)
        z = outer_inverse_gate(tabs["wi_re"], tabs["wi_im"], g, gate, z,
                               bias[order].reshape(1, HY_W).astype(F32), token_major_out=(order == 1))
    return z.reshape(seq, HY_W)


def _chunk_rows(c, t):
    return pl.ds(pl.multiple_of(c * t, t), t)


def _retention_kernel(lg_ref, q_ref, k_ref, v_ref, g_ref, o_ref, rstore_ref, s_ref, *, nblk, cpb, t):
    h = pl.program_id(0)
    sweep = pl.program_id(1)
    i = pl.program_id(2)
    lgf = lg_ref[0, h]
    lgb = lg_ref[1, h]
    pos = lax.broadcasted_iota(jnp.int32, (t, 1), 0).astype(F32)
    chunk_len = jnp.full((1, RET_DV), float(t), F32)

    @pl.when(i == 0)
    def _():
        s_ref[...] = jnp.zeros_like(s_ref)

    @pl.when(sweep == 0)
    def _():
        def body(c, carry):
            cc = cpb - 1 - c
            rows = _chunk_rows(cc, t)
            rstore_ref[(nblk - 1 - i) * cpb + cc] = s_ref[...]
            kw = (k_ref[rows, :].astype(F32) * jnp.exp(lgb * pos)).astype(BF16)
            s_ref[...] = s_ref[...] * jnp.exp(lgb * chunk_len) + _dot_tn(kw, v_ref[rows, :])
            return carry

        lax.fori_loop(0, cpb, body, 0)

    @pl.when(sweep == 1)
    def _():
        ri = lax.broadcasted_iota(jnp.int32, (t, t), 0)
        ci = lax.broadcasted_iota(jnp.int32, (t, t), 1)
        diff = (ri - ci).astype(F32)
        decay = (jnp.where(diff >= 0, jnp.exp(lgf * jnp.maximum(diff, 0.0)), 0.0)
                 + jnp.where(diff <= 0, jnp.exp(lgb * jnp.maximum(-diff, 0.0)), 0.0))
        q_fwd = jnp.exp(lgf * (pos + 1.0)) * (RET_DK ** -0.5)
        q_bwd = jnp.exp(lgb * (t - pos)) * (RET_DK ** -0.5)
        k_fwd = jnp.exp(lgf * (t - 1.0 - pos))

        def body(c, carry):
            rows = _chunk_rows(c, t)
            q = q_ref[rows, :].astype(F32)
            k = k_ref[rows, :]
            v = v_ref[rows, :]
            scores = _dot_nt((q * (RET_DK ** -0.5)).astype(BF16), k) * decay
            y = _dot(scores.astype(BF16), v)
            y = y + _dot((q * q_fwd).astype(BF16), s_ref[...].astype(BF16))
            y = y + _dot((q * q_bwd).astype(BF16), rstore_ref[i * cpb + c].astype(BF16))
            kw = (k.astype(F32) * k_fwd).astype(BF16)
            s_ref[...] = s_ref[...] * jnp.exp(lgf * chunk_len) + _dot_tn(kw, v)
            y = y * lax.rsqrt(jnp.mean(y * y, axis=-1, keepdims=True) + EPS)
            gt = g_ref[rows, :].astype(F32)
            o_ref[rows, :] = (gt * jax.nn.sigmoid(gt) * y).astype(o_ref.dtype)
            return carry

        lax.fori_loop(0, cpb, body, 0)


CHUNKS_PER_STEP = 4


def retention_mixer(proj, col0, log_decay, t=CHUNK):
    seq = proj.shape[0]
    nc = seq // t
    cpb = math.gcd(CHUNKS_PER_STEP, nc)
    nblk = nc // cpb
    tb = cpb * t
    qb = col0 // RET_DK
    kb = qb + RET_H
    vb = (col0 + 2 * RET_H * RET_DK) // RET_DV
    gb = vb + RET_H

    def rows(sweep, i):
        return sweep * i + (1 - sweep) * (nblk - 1 - i)

    return pl.pallas_call(
        functools.partial(_retention_kernel, nblk=nblk, cpb=cpb, t=t),
        grid=(RET_H, 2, nblk),
        in_specs=[pl.BlockSpec(memory_space=pltpu.SMEM),
                  pl.BlockSpec((tb, RET_DK), lambda h, s, i: (s * i, qb + h)),
                  pl.BlockSpec((tb, RET_DK), lambda h, s, i: (rows(s, i), kb + h)),
                  pl.BlockSpec((tb, RET_DV), lambda h, s, i: (rows(s, i), vb + h)),
                  pl.BlockSpec((tb, RET_DV), lambda h, s, i: (s * i, gb + h))],
        out_specs=pl.BlockSpec((tb, RET_DV), lambda h, s, i: (s * i, h)),
        out_shape=jax.ShapeDtypeStruct((seq, RET_H * RET_DV), BF16),
        scratch_shapes=[pltpu.VMEM((nc, RET_DK, RET_DV), F32), pltpu.VMEM((RET_DK, RET_DV), F32)],
        compiler_params=_cparams("arbitrary", "arbitrary", "arbitrary"),
        name="retention",
    )(log_decay, proj, proj, proj, proj)


def _log_sigmoid(x):
    return jnp.minimum(x, 0.0) - jnp.log(1.0 + jnp.exp(-jnp.abs(x)))


def _mlstm_gates(gc_ref, gr_ref, bias_ref, h, direction, rows):
    bi = bias_ref[direction * 2 * ML_H + h]
    bf = bias_ref[direction * 2 * ML_H + ML_H + h]
    a = 2 * direction
    ig_c = gc_ref[0, rows, a:a + 1] + bi
    lf_c = _log_sigmoid(gc_ref[0, rows, a + 1:a + 2] + bf)
    ig_r = gr_ref[0, a:a + 1, rows] + bi
    lf_r = _log_sigmoid(gr_ref[0, a + 1:a + 2, rows] + bf)
    return ig_c, lf_c, ig_r, lf_r


def _split3(x):
    hi = x.astype(BF16)
    rest = x - hi.astype(F32)
    mid = rest.astype(BF16)
    return hi, mid, (rest - mid.astype(F32)).astype(BF16)


def _running_sums(lf_fwd, lf_bwd, as_rows):
    t = lf_fwd.shape[1] if as_rows else lf_fwd.shape[0]
    ri = lax.broadcasted_iota(jnp.int32, (t, t), 0)
    ci = lax.broadcasted_iota(jnp.int32, (t, t), 1)
    tri = (ri >= ci).astype(BF16)
    if as_rows:
        sel = lax.broadcasted_iota(jnp.int32, (8, t), 0)
        both = jnp.where(sel == 0, lf_fwd, jnp.where(sel == 1, lf_bwd, 0.0))
        left = sum(_dot_nt(p, tri) for p in _split3(both))
        left_f, left_b = left[0:1, :], left[1:2, :]
        total_b = jnp.sum(lf_bwd, axis=1, keepdims=True)
    else:
        sel = lax.broadcasted_iota(jnp.int32, (t, LANES), 1)
        both = jnp.where(sel == 0, lf_fwd, jnp.where(sel == 1, lf_bwd, 0.0))
        left = sum(_dot(tri, p) for p in _split3(both))
        left_f, left_b = left[:, 0:1], left[:, 1:2]
        total_b = jnp.sum(lf_bwd, axis=0, keepdims=True)
    return left_f, total_b - left_b + lf_bwd


def _mlstm_state_step(k, v, ig_c, cum_c, total, c_ref, n_ref, m_ref):
    a = total - cum_c + ig_c
    m_loc = jnp.max(a, axis=0, keepdims=True)
    kw = k * jnp.exp(a - m_loc)
    kv = _dot_tn(kw.astype(BF16), v)
    ksum = jnp.sum(kw, axis=0, keepdims=True)
    m_old = m_ref[0:1, 0:1]
    m_new = jnp.maximum(total + m_old, m_loc)
    sp = jnp.exp(total + m_old - m_new)
    sc = jnp.exp(m_loc - m_new)
    c_ref[...] = sp * c_ref[...] + sc * kv
    n_ref[...] = sp * n_ref[...] + sc * jnp.broadcast_to(ksum, n_ref.shape)
    m_ref[...] = jnp.broadcast_to(m_new, m_ref.shape)


def _mlstm_output(qk, q, v, ig_r, cum_c, cum_r, c_prev, n_prev, m_prev, backward):
    t = q.shape[0]
    ri = lax.broadcasted_iota(jnp.int32, (t, t), 0)
    ci = lax.broadcasted_iota(jnp.int32, (t, t), 1)
    keep = (ri <= ci) if backward else (ri >= ci)
    dlog = jnp.where(keep, cum_c - cum_r + ig_r, -jnp.inf)
    inter = cum_c + m_prev
    m_t = jnp.maximum(inter, jnp.max(dlog, axis=-1, keepdims=True))
    s = qk * jnp.exp(dlog - m_t)
    wi = jnp.exp(inter - m_t)
    num = _dot(s.astype(BF16), v) + wi * _dot(q.astype(BF16), c_prev.astype(BF16))
    den = jnp.sum(s, axis=-1, keepdims=True) + wi * jnp.sum(q * n_prev, axis=-1, keepdims=True)
    return num / jnp.maximum(jnp.abs(den), jnp.exp(-m_t))


def _mlstm_kernel(bias_ref, q_ref, k_ref, v_ref, o_ref, gc_ref, gr_ref, gain_ref, out_ref,
                  cstore_ref, nstore_ref, mstore_ref, c_ref, n_ref, m_ref, *, nblk, cpb, t):
    h = pl.program_id(0)
    sweep = pl.program_id(1)
    i = pl.program_id(2)

    @pl.when(i == 0)
    def _():
        c_ref[...] = jnp.zeros_like(c_ref)
        n_ref[...] = jnp.zeros_like(n_ref)
        m_ref[...] = jnp.zeros_like(m_ref)

    @pl.when(sweep == 0)
    def _():
        def body(c, carry):
            cc = cpb - 1 - c
            rows = _chunk_rows(cc, t)
            n = (nblk - 1 - i) * cpb + cc
            cstore_ref[n] = c_ref[...]
            nstore_ref[n] = n_ref[...]
            mstore_ref[n] = m_ref[...]
            k = k_ref[rows, :].astype(F32) * (ML_DK ** -0.5)
            _, lf_c, _, _ = _mlstm_gates(gc_ref, gr_ref, bias_ref, h, 0, rows)
            ig_c, lb_c, _, _ = _mlstm_gates(gc_ref, gr_ref, bias_ref, h, 1, rows)
            _, cumb_c = _running_sums(lf_c, lb_c, False)
            total = jnp.sum(lb_c, axis=0, keepdims=True)
            _mlstm_state_step(k, v_ref[rows, :], ig_c, cumb_c, total, c_ref, n_ref, m_ref)
            return carry

        lax.fori_loop(0, cpb, body, 0)

    @pl.when(sweep == 1)
    def _():
        def body(c, carry):
            rows = _chunk_rows(c, t)
            n = i * cpb + c
            k = k_ref[rows, :].astype(F32) * (ML_DK ** -0.5)
            v = v_ref[rows, :]
            q = q_ref[rows, :].astype(F32)
            qk = _dot_nt(q_ref[rows, :], k.astype(BF16))
            ig_c, lf_c, ig_r, lf_r = _mlstm_gates(gc_ref, gr_ref, bias_ref, h, 0, rows)
            _, lb_c, igb_r, lb_r = _mlstm_gates(gc_ref, gr_ref, bias_ref, h, 1, rows)
            cum_c, cumb_c = _running_sums(lf_c, lb_c, False)
            cum_r, cumb_r = _running_sums(lf_r, lb_r, True)
            hf = _mlstm_output(qk, q, v, ig_r, cum_c, cum_r, c_ref[...], n_ref[0:1, :], m_ref[0:1, 0:1], False)
            total = jnp.sum(lf_c, axis=0, keepdims=True)
            _mlstm_state_step(k, v, ig_c, cum_c, total, c_ref, n_ref, m_ref)
            hb = _mlstm_output(qk, q, v, igb_r, cumb_c, cumb_r, cstore_ref[n], nstore_ref[n][0:1, :],
                               mstore_ref[n][0:1, 0:1], True)
            y = hf + hb
            y = y * lax.rsqrt(jnp.mean(y * y, axis=-1, keepdims=True) + EPS) * gain_ref[...]
            out_ref[rows, :] = (jax.nn.sigmoid(o_ref[rows, :].astype(F32)) * y).astype(out_ref.dtype)
            return carry

        lax.fori_loop(0, cpb, body, 0)


def mlstm_mixer(proj, gates, col0, gate_bias, norm_gain, t=CHUNK):
    seq = proj.shape[0]
    nc = seq // t
    cpb = math.gcd(CHUNKS_PER_STEP, nc)
    nblk = nc // cpb
    tb = cpb * t
    qb = col0 // ML_DK
    kb = qb + ML_H
    vb = (col0 + 2 * ML_H * ML_DK) // ML_DV
    ob = vb + ML_H
    g = gates[:, :4 * ML_H].reshape(seq, 2, 2, ML_H)
    g = jnp.transpose(g, (3, 0, 1, 2)).reshape(ML_H, seq, 4)
    g_cols = g
    g_rows = jnp.transpose(g, (0, 2, 1))

    def rows(sweep, i):
        return sweep * i + (1 - sweep) * (nblk - 1 - i)

    return pl.pallas_call(
        functools.partial(_mlstm_kernel, nblk=nblk, cpb=cpb, t=t),
        grid=(ML_H, 2, nblk),
        in_specs=[pl.BlockSpec(memory_space=pltpu.SMEM),
                  pl.BlockSpec((tb, ML_DK), lambda h, s, i: (s * i, qb + h)),
                  pl.BlockSpec((tb, ML_DK), lambda h, s, i: (rows(s, i), kb + h)),
                  pl.BlockSpec((tb, ML_DV), lambda h, s, i: (rows(s, i), vb + h)),
                  pl.BlockSpec((tb, ML_DV), lambda h, s, i: (s * i, ob + h)),
                  pl.BlockSpec((1, tb, 4), lambda h, s, i: (h, rows(s, i), 0)),
                  pl.BlockSpec((1, 4, tb), lambda h, s, i: (h, 0, rows(s, i))),
                  pl.BlockSpec((1, ML_DV), lambda h, s, i: (0, h))],
        out_specs=pl.BlockSpec((tb, ML_DV), lambda h, s, i: (s * i, h)),
        out_shape=jax.ShapeDtypeStruct((seq, ML_H * ML_DV), BF16),
        scratch_shapes=[pltpu.VMEM((nc, ML_DK, ML_DV), F32), pltpu.VMEM((nc, 8, ML_DK), F32),
                        pltpu.VMEM((nc, 8, LANES), F32), pltpu.VMEM((ML_DK, ML_DV), F32),
                        pltpu.VMEM((8, ML_DK), F32), pltpu.VMEM((8, LANES), F32)],
        compiler_params=_cparams("arbitrary", "arbitrary", "arbitrary"),
        name="mlstm",
    )(gate_bias, proj, proj, proj, proj, g_cols, g_rows, norm_gain.reshape(1, ML_H * ML_DV))


PERM_ROWS = 256
CLASS_RUN = 16
ATT_SUB_ROWS = 128


def _group_permutation(dilation):
    run = CLASS_RUN * dilation
    new = jnp.arange(PERM_ROWS, dtype=jnp.int32)
    within = new % run
    src = (new // run) * run + (within % CLASS_RUN) * dilation + within // CLASS_RUN
    return (src[:, None] == jnp.arange(PERM_ROWS, dtype=jnp.int32)[None, :]).astype(BF16)


def _attention_prep_kernel(gain_ref, p4_ref, p16_ref, x_ref, o1_ref, o4_ref, o16_ref):
    part = pl.program_id(1)

    @pl.when(part < 2)
    def _():
        gain = gain_ref[0]
        for h in range(ATT_H):
            cols = slice(h * ATT_DH, (h + 1) * ATT_DH)
            x = x_ref[:, cols].astype(F32)
            o1_ref[:, cols] = (x * lax.rsqrt(jnp.mean(x * x, axis=-1, keepdims=True) + EPS) * gain).astype(BF16)

    @pl.when(part == 2)
    def _():
        o1_ref[...] = x_ref[...]

    for s in range(x_ref.shape[0] // PERM_ROWS):
        rows = slice(s * PERM_ROWS, (s + 1) * PERM_ROWS)
        x = o1_ref[rows, :]
        o4_ref[rows, :] = _dot(p4_ref[...], x).astype(BF16)
        o16_ref[rows, :] = _dot(p16_ref[...], x).astype(BF16)


def attention_prep(proj, col0, qk_gain, tr=512):
    seq = proj.shape[0]
    cb = col0 // W_GROUP
    gains = jnp.stack([qk_gain[0] * (ATT_DH ** -0.5), qk_gain[1], jnp.ones_like(qk_gain[0])]).reshape(3, 1, ATT_DH)
    out = jax.ShapeDtypeStruct((seq, 3 * W_GROUP), BF16)
    ospec = pl.BlockSpec((tr, W_GROUP), lambda i, j: (i, j))
    pspec = pl.BlockSpec((PERM_ROWS, PERM_ROWS), lambda i, j: (0, 0))
    return pl.pallas_call(
        _attention_prep_kernel,
        grid=(seq // tr, 3),
        in_specs=[pl.BlockSpec((1, 1, ATT_DH), lambda i, j: (j, 0, 0)), pspec, pspec,
                  pl.BlockSpec((tr, W_GROUP), lambda i, j: (i, cb + j))],
        out_specs=[ospec, ospec, ospec],
        out_shape=[out, out, out],
        compiler_params=_cparams("parallel", "arbitrary"),
        name="attention_prep",
    )(gains.astype(F32), _group_permutation(4), _group_permutation(16), proj)


def _band_attention_kernel(q_ref, kp_ref, kc_ref, kn_ref, vp_ref, vc_ref, vn_ref, o_ref, lse_ref,
                           *, dilation, nblk):
    i = pl.program_id(1)
    tq = q_ref.shape[0] * CLASS_RUN
    hs = ATT_HALF_STEPS
    sq = min(ATT_SUB_ROWS, tq)
    sk = sq + 2 * hs
    ri = lax.broadcasted_iota(jnp.int32, (sq, sk), 0)
    ci = lax.broadcasted_iota(jnp.int32, (sq, sk), 1)
    off = ci - hs - ri
    first_col = jnp.where(i > 0, 0, hs)
    end_col = jnp.where(i < nblk - 1, tq + 2 * hs, tq + hs)
    in_band = jnp.abs(off) <= hs
    dist = (jnp.abs(off) * dilation).astype(F32)
    valid = [in_band & (ci + s0 >= first_col) & (ci + s0 < end_col) for s0 in range(0, tq, sq)]
    lane = lax.broadcasted_iota(jnp.int32, (sq, LANES), 1)
    lse_all = [jnp.zeros((sq, LANES), F32) for _ in valid]

    def rows(ref, cols):
        x = ref[:, :, cols]
        return x.reshape(x.shape[0] * CLASS_RUN, x.shape[2])

    for h in range(ATT_H):
        cols = slice(h * ATT_DH, (h + 1) * ATT_DH)
        slope = 2.0 ** (-8.0 * (h + 1) / ATT_H)
        qq = rows(q_ref, cols)
        kk = jnp.concatenate([rows(kp_ref, cols), rows(kc_ref, cols), rows(kn_ref, cols)], axis=0)
        vv = jnp.concatenate([rows(vp_ref, cols), rows(vc_ref, cols), rows(vn_ref, cols)], axis=0)
        outs = []
        for b, s0 in enumerate(range(0, tq, sq)):
            s = _dot_nt(qq[s0:s0 + sq], kk[s0:s0 + sk]) - slope * dist
            s = jnp.where(valid[b], s, NEG)
            m = jnp.max(s, axis=-1, keepdims=True)
            p = jnp.exp(s - m)
            den = jnp.sum(p, axis=-1, keepdims=True)
            outs.append(_dot(p.astype(BF16), vv[s0:s0 + sk]) / den)
            lse_all[b] = jnp.where(lane == h, m + jnp.log(den), lse_all[b])
        o = jnp.concatenate(outs, axis=0)
        o_ref[:, :, cols] = o.astype(o_ref.dtype).reshape(tq // CLASS_RUN, CLASS_RUN, ATT_DH)
    lse_ref[...] = jnp.concatenate(lse_all, axis=0).reshape(tq // CLASS_RUN, CLASS_RUN, LANES)


def band_attention(qkv, dilation, tq=256):
    seq = qkv.shape[0]
    n = seq // dilation
    tq = min(tq, n)
    nblk = n // tq
    hs = ATT_HALF_STEPS
    runs = n // CLASS_RUN
    tr = tq // CLASS_RUN
    hr = hs // CLASS_RUN
    ratio = tq // hs
    last_halo = n // hs - 1
    view = qkv.reshape(runs, dilation, CLASS_RUN, 3 * W_GROUP)

    def cur(part):
        return pl.BlockSpec((tr, None, CLASS_RUN, W_GROUP), lambda r, i: (i, r, 0, part))

    def prev(part):
        return pl.BlockSpec((hr, None, CLASS_RUN, W_GROUP), lambda r, i: (jnp.maximum(i * ratio - 1, 0), r, 0, part))

    def nxt(part):
        return pl.BlockSpec((hr, None, CLASS_RUN, W_GROUP),
                            lambda r, i: (jnp.minimum((i + 1) * ratio, last_halo), r, 0, part))

    o, lse = pl.pallas_call(
        functools.partial(_band_attention_kernel, dilation=dilation, nblk=nblk),
        grid=(dilation, nblk),
        in_specs=[cur(0), prev(1), cur(1), nxt(1), prev(2), cur(2), nxt(2)],
        out_specs=[pl.BlockSpec((tr, None, CLASS_RUN, W_GROUP), lambda r, i: (i, r, 0, 0)),
                   pl.BlockSpec((tr, None, CLASS_RUN, LANES), lambda r, i: (i, r, 0, 0))],
        out_shape=[jax.ShapeDtypeStruct((runs, dilation, CLASS_RUN, W_GROUP), BF16),
                   jax.ShapeDtypeStruct((runs, dilation, CLASS_RUN, LANES), F32)],
        compiler_params=_cparams("parallel", "arbitrary"),
        name=f"band_attention_d{dilation}",
    )(view, view, view, view, view, view, view)
    return o.reshape(seq, W_GROUP), lse.reshape(seq, LANES)


def _merge_branches_kernel(q4_ref, q16_ref, o1_ref, o4_ref, o16_ref, l1_ref, l4_ref, l16_ref, out_ref):
    def ungroup(qt, x):
        return _dot(qt, x)

    def ungroup_f32(qt, x):
        hi = x.astype(BF16)
        lo = (x - hi.astype(F32)).astype(BF16)
        return _dot(qt, hi) + _dot(qt, lo)

    for s in range(o1_ref.shape[0] // PERM_ROWS):
        rows = slice(s * PERM_ROWS, (s + 1) * PERM_ROWS)
        q4, q16 = q4_ref[...], q16_ref[...]
        l1 = l1_ref[rows, :]
        l2 = ungroup_f32(q4, l4_ref[rows, :])
        l3 = ungroup_f32(q16, l16_ref[rows, :])
        m = jnp.maximum(jnp.maximum(l1, l2), l3)
        e1, e2, e3 = jnp.exp(l1 - m), jnp.exp(l2 - m), jnp.exp(l3 - m)
        inv = 1.0 / (e1 + e2 + e3)
        w1, w2, w3 = e1 * inv, e2 * inv, e3 * inv
        o2 = ungroup(q4, o4_ref[rows, :])
        o3 = ungroup(q16, o16_ref[rows, :])
        for h in range(ATT_H):
            cols = slice(h * ATT_DH, (h + 1) * ATT_DH)
            out_ref[rows, cols] = (w1[:, h:h + 1] * o1_ref[rows, cols].astype(F32)
                                   + w2[:, h:h + 1] * o2[:, cols]
                                   + w3[:, h:h + 1] * o3[:, cols]).astype(out_ref.dtype)


def merge_branches(outs, lses, tr=512):
    seq = outs[0].shape[0]
    ospec = pl.BlockSpec((tr, W_GROUP), lambda i: (i, 0))
    lspec = pl.BlockSpec((tr, LANES), lambda i: (i, 0))
    pspec = pl.BlockSpec((PERM_ROWS, PERM_ROWS), lambda i: (0, 0))
    return pl.pallas_call(
        _merge_branches_kernel,
        grid=(seq // tr,),
        in_specs=[pspec, pspec, ospec, ospec, ospec, lspec, lspec, lspec],
        out_specs=ospec,
        out_shape=jax.ShapeDtypeStruct((seq, W_GROUP), BF16),
        compiler_params=_cparams("parallel"),
        name="attention_merge",
    )(_group_permutation(4).T, _group_permutation(16).T, *outs, *lses)


def dilated_attention(proj, col0, qk_gain):
    grouped = attention_prep(proj, col0, qk_gain)
    outs, lses = [], []
    for d, qkv in zip(ATT_DILATIONS, grouped):
        o, lse = band_attention(qkv, d)
        outs.append(o)
        lses.append(lse)
    return merge_branches(outs, lses)


def kernel(x, c, ada_w, ada_b, ada_table, w_in, w_out, hy_short, hy_w1, hy_b1, hy_w2, hy_b2, hy_w3, hy_bias,
           ret_decay, att_qk_gain, ml_gate_bias, ml_norm_gain, ffn_w1, ffn_w3, ffn_w2):
    batch, seq, d_model = x.shape
    depth = w_in.shape[0]
    d_main = 12 * W_GROUP
    hidden = ffn_w1.shape[2]

    tabs = _dft_tables(seq)
    consts = _filter_positions(seq)
    mod_shared = ada_modulation(c, ada_w, ada_b)

    w_in_b = w_in.astype(BF16)
    w_gate_b = jnp.pad(w_in[:, :, d_main:], ((0, 0), (0, 0), (0, LANES - (w_in.shape[2] - d_main)))).astype(BF16)
    w_out_b = w_out.astype(BF16)
    w1_b = ffn_w1.astype(BF16)
    w3_b = ffn_w3.astype(BF16)
    w2_b = ffn_w2.astype(BF16)

    rows = x.reshape(batch * seq, d_model)
    outs = []
    for b in range(batch):
        xb = rows[b * seq:(b + 1) * seq]
        for l in range(depth):
            mod = (mod_shared[b:b + 1] + ada_table[l].reshape(1, -1)).reshape(6, d_model)
            sh1, sc1, g1, sh2, sc2, g2 = (mod[i:i + 1] for i in range(6))
            proj, gates = norm_proj(xb, 1.0 + sc1, sh1, w_in_b, l, d_main, w_gate_b[l])

            kf = hyena_filter_spectra(seq, tabs, consts, hy_w1[l], hy_b1[l], hy_w2[l], hy_b2[l], hy_w3[l])
            y_a = hyena_mixer(proj, tabs, kf, hy_short[l], hy_bias[l])
            y_b = retention_mixer(proj, 3 * W_GROUP, jax.nn.log_sigmoid(ret_decay[l].astype(F32)))
            y_c = dilated_attention(proj, 6 * W_GROUP, att_qk_gain[l])
            y_d = mlstm_mixer(proj, gates, 9 * W_GROUP, ml_gate_bias[l], ml_norm_gain[l])
            y = jnp.concatenate([y_a.astype(BF16), y_b, y_c, y_d], axis=-1)
            xb = mm_residual(y, w_out_b, l, xb, g1, tm=1024, tn=512, tk=y.shape[1])

            u = norm_swiglu(xb, 1.0 + sc2, sh2, w1_b, w3_b, l)
            xb = mm_residual(u, w2_b, l, xb, g2, tm=1024, tn=512, tk=hidden // 2)
        outs.append(xb)
    return jnp.concatenate(outs, 0).reshape(batch, seq, d_model)
```

```python
import functools
import math

import numpy as np
import jax
import jax.numpy as jnp
from jax import lax
from jax.experimental import pallas as pl
from jax.experimental.pallas import tpu as pltpu

F32 = jnp.float32
BF16 = jnp.bfloat16
HIGHEST = lax.Precision.HIGHEST

EPS = 1e-6
NEG = -1e30

V7X_VMEM_LIMIT_BYTES = 56 * 1024 * 1024
LANES = 128
BF16_TILE_ROWS = 16

W_GROUP = 1024
RET_H, RET_DK, RET_DV = 4, 128, 256
ATT_H, ATT_DH = 8, 128
ATT_HALF_STEPS = 64
ATT_DILATIONS = (1, 4, 16)
ML_H, ML_DK, ML_DV = 4, 128, 256
HY_BANDS = 16
HY_W = W_GROUP
HY_FILTER_WIDTH = 64
HY_TARGET, HY_SHORT_PCT, HY_LONG_PCT = 1e-2, 0.3, 1.5
HY_FEAT_PAD = 40
DFT_N2 = 128
CHUNK = 256


def _cparams(*sem):
    return pltpu.CompilerParams(dimension_semantics=sem, vmem_limit_bytes=V7X_VMEM_LIMIT_BYTES)


def _dot(a, b):
    return jnp.dot(a, b, preferred_element_type=F32)


def _dot_nt(a, b):
    return lax.dot_general(a, b, (((1,), (1,)), ((), ())), preferred_element_type=F32)


def _dot_tn(a, b):
    return lax.dot_general(a, b, (((0,), (0,)), ((), ())), preferred_element_type=F32)


def _ada_kernel(c_ref, w_ref, b_ref, o_ref):
    c = c_ref[...]
    s = c * jax.nn.sigmoid(c)
    o_ref[...] = jnp.dot(s, w_ref[...], preferred_element_type=F32, precision=HIGHEST) + b_ref[...]


def ada_modulation(c, ada_w, ada_b):
    d, n = ada_w.shape
    tn = 512
    c8 = jnp.broadcast_to(c.reshape(1, d), (8, d))
    out = pl.pallas_call(
        _ada_kernel,
        grid=(n // tn,),
        in_specs=[pl.BlockSpec((8, d), lambda j: (0, 0)),
                  pl.BlockSpec((d, tn), lambda j: (0, j)),
                  pl.BlockSpec((1, tn), lambda j: (0, j))],
        out_specs=pl.BlockSpec((8, tn), lambda j: (0, j)),
        out_shape=jax.ShapeDtypeStruct((8, n), F32),
        compiler_params=_cparams("parallel"),
        name="ada_modulation",
    )(c8, ada_w, ada_b.reshape(1, n))
    return out[0:1]


NORM_ROWS = 16


def _normalise_into(x_ref, sc_ref, sh_ref, h_ref):
    tm = x_ref.shape[0]
    sc = sc_ref[...]
    sh = sh_ref[...]

    def body(r, carry):
        rows = pl.ds(pl.multiple_of(r * NORM_ROWS, NORM_ROWS), NORM_ROWS)
        x = x_ref[rows, :]
        ms = jnp.mean(x * x, axis=-1, keepdims=True)
        h_ref[rows, :] = (x * lax.rsqrt(ms + EPS) * sc + sh).astype(h_ref.dtype)
        return carry

    lax.fori_loop(0, tm // NORM_ROWS, body, 0, unroll=4)


def _norm_proj_kernel(x_ref, sc_ref, sh_ref, w_ref, wg_ref, o_ref, g_ref, h_ref):
    @pl.when(pl.program_id(1) == 0)
    def _():
        _normalise_into(x_ref, sc_ref, sh_ref, h_ref)
        g_ref[...] = _dot(h_ref[...], wg_ref[...])

    o_ref[...] = _dot(h_ref[...], w_ref[...]).astype(o_ref.dtype)


def norm_proj(x, scale1p, shift, w_stack, layer, n, wg, tm=1024, tn=1024):
    m, d = x.shape
    ng = wg.shape[1]
    return pl.pallas_call(
        _norm_proj_kernel,
        grid=(m // tm, n // tn),
        in_specs=[pl.BlockSpec((tm, d), lambda i, j: (i, 0), pipeline_mode=pl.Buffered(1)),
                  pl.BlockSpec((1, d), lambda i, j: (0, 0)),
                  pl.BlockSpec((1, d), lambda i, j: (0, 0)),
                  pl.BlockSpec((None, d, tn), lambda i, j: (layer, 0, j)),
                  pl.BlockSpec((d, ng), lambda i, j: (0, 0))],
        out_specs=[pl.BlockSpec((tm, tn), lambda i, j: (i, j)),
                   pl.BlockSpec((tm, ng), lambda i, j: (i, 0))],
        out_shape=[jax.ShapeDtypeStruct((m, n), BF16), jax.ShapeDtypeStruct((m, ng), F32)],
        scratch_shapes=[pltpu.VMEM((tm, d), BF16)],
        compiler_params=_cparams("parallel", "arbitrary"),
        name="norm_proj",
    )(x, scale1p, shift, w_stack, wg)


def _norm_swiglu_kernel(x_ref, sc_ref, sh_ref, w1_ref, w3_ref, o_ref, h_ref):
    @pl.when(pl.program_id(1) == 0)
    def _():
        _normalise_into(x_ref, sc_ref, sh_ref, h_ref)

    h = h_ref[...]
    a = _dot(h, w1_ref[...])
    b = _dot(h, w3_ref[...])
    o_ref[...] = (a * jax.nn.sigmoid(a) * b).astype(o_ref.dtype)


def norm_swiglu(x, scale1p, shift, w1_stack, w3_stack, layer, tm=1024, tn=256):
    m, d = x.shape
    n = w1_stack.shape[2]
    return pl.pallas_call(
        _norm_swiglu_kernel,
        grid=(m // tm, n // tn),
        in_specs=[pl.BlockSpec((tm, d), lambda i, j: (i, 0)),
                  pl.BlockSpec((1, d), lambda i, j: (0, 0)),
                  pl.BlockSpec((1, d), lambda i, j: (0, 0)),
                  pl.BlockSpec((None, d, tn), lambda i, j: (layer, 0, j)),
                  pl.BlockSpec((None, d, tn), lambda i, j: (layer, 0, j))],
        out_specs=pl.BlockSpec((tm, tn), lambda i, j: (i, j)),
        out_shape=jax.ShapeDtypeStruct((m, n), BF16),
        scratch_shapes=[pltpu.VMEM((tm, d), BF16)],
        compiler_params=_cparams("parallel", "arbitrary"),
        name="norm_swiglu",
    )(x, scale1p, shift, w1_stack, w3_stack)


def _mm_residual_kernel(y_ref, w_ref, x_ref, g_ref, o_ref, acc_ref, *, nk):
    k = pl.program_id(2)
    part = _dot(y_ref[...], w_ref[...])

    @pl.when(k == 0)
    def _():
        acc_ref[...] = part

    @pl.when(k > 0)
    def _():
        acc_ref[...] += part

    @pl.when(k == nk - 1)
    def _():
        o_ref[...] = x_ref[...] + g_ref[...] * acc_ref[...]


def _mm_residual_1k_kernel(y_ref, w_ref, x_ref, g_ref, o_ref):
    o_ref[...] = x_ref[...] + g_ref[...] * _dot(y_ref[...], w_ref[...])


def mm_residual(y, w_stack, layer, x, gate, tm, tn, tk):
    m, kk = y.shape
    n = w_stack.shape[2]
    nk = kk // tk
    if nk == 1:
        return pl.pallas_call(
            _mm_residual_1k_kernel,
            grid=(m // tm, n // tn),
            in_specs=[pl.BlockSpec((tm, kk), lambda i, j: (i, 0)),
                      pl.BlockSpec((None, kk, tn), lambda i, j: (layer, 0, j)),
                      pl.BlockSpec((tm, tn), lambda i, j: (i, j)),
                      pl.BlockSpec((1, tn), lambda i, j: (0, j))],
            out_specs=pl.BlockSpec((tm, tn), lambda i, j: (i, j)),
            out_shape=jax.ShapeDtypeStruct((m, n), F32),
            compiler_params=_cparams("parallel", "arbitrary"),
            name="mm_residual",
        )(y, w_stack, x, gate)
    return pl.pallas_call(
        functools.partial(_mm_residual_kernel, nk=nk),
        grid=(m // tm, n // tn, nk),
        in_specs=[pl.BlockSpec((tm, tk), lambda i, j, k: (i, k)),
                  pl.BlockSpec((None, tk, tn), lambda i, j, k: (layer, k, j)),
                  pl.BlockSpec((tm, tn), lambda i, j, k: (i, j)),
                  pl.BlockSpec((1, tn), lambda i, j, k: (0, j))],
        out_specs=pl.BlockSpec((tm, tn), lambda i, j, k: (i, j)),
        out_shape=jax.ShapeDtypeStruct((m, n), F32),
        scratch_shapes=[pltpu.VMEM((tm, tn), F32)],
        compiler_params=_cparams("parallel", "arbitrary", "arbitrary"),
        name="mm_residual_ksplit",
    )(y, w_stack, x, gate)


def _dft_tables(seq_len):
    n = 2 * seq_len
    n2 = DFT_N2
    n1 = n // n2
    half = n1 // 2
    kb = half + 1
    kbp = -(-kb // 8) * 8
    k1 = jnp.arange(kbp, dtype=jnp.int32)
    live = (k1 <= half)
    col = jnp.arange(n1, dtype=jnp.int32)
    ph = (2.0 * math.pi / n1) * ((k1[:, None] * col[None, :]) % n1).astype(F32)
    wf = jnp.stack([jnp.cos(ph), -jnp.sin(ph)], axis=0) * live[None, :, None]
    wf = wf.reshape(2 * kbp, n1)
    a = jnp.arange(n2, dtype=jnp.int32)
    mm = (a[None, None, :] * (k1[:, None, None] + n1 * a[None, :, None])) % n
    th = (2.0 * math.pi / n) * mm.astype(F32)
    c, s = jnp.cos(th), jnp.sin(th)
    tf = jnp.concatenate([jnp.concatenate([c, s], 2), jnp.concatenate([-s, c], 2)], 1)
    tf = tf * live[:, None, None]
    ti = jnp.transpose(tf, (0, 2, 1))
    wt = jnp.where((k1 == 0) | (k1 == half), 1.0, 2.0) * live / n
    row = jnp.arange(half, dtype=jnp.int32)
    ph2 = (2.0 * math.pi / n1) * ((row[:, None] * k1[None, :]) % n1).astype(F32)
    return dict(wf_data=wf[:, :half].astype(BF16), wf_filt=wf.astype(BF16), tf=tf.astype(BF16),
                ti=ti.astype(BF16), wi_re=(wt * jnp.cos(ph2)).astype(BF16), wi_im=(-wt * jnp.sin(ph2)).astype(BF16),
                kbp=kbp, n1=n1)


INNER_STEP = 8


U32 = jnp.uint32


def _pack_pair(hi, lo):
    hb = lax.bitcast_convert_type(hi.astype(BF16).astype(F32), U32)
    lb = lax.bitcast_convert_type(lo.astype(BF16).astype(F32), U32)
    return hb | (lb >> 16)


def _unpack_pair(w):
    hi = lax.bitcast_convert_type(w & jnp.uint32(0xFFFF0000), F32).astype(BF16)
    lo = lax.bitcast_convert_type(w << 16, F32).astype(BF16)
    return hi, lo


def _outer_dft_kernel(w_ref, x_ref, o_ref, *, packed_in):
    kbp = o_ref.shape[0]
    w = w_ref[...]
    for j in range(x_ref.shape[0]):
        if packed_in:
            parts = _unpack_pair(x_ref[j])
        else:
            parts = (x_ref[j].astype(BF16),)
        tc = parts[0].shape[1]
        for p, x in enumerate(parts):
            a = _dot(w, x)
            o_ref[:, j, p * tc:(p + 1) * tc] = _pack_pair(a[:kbp], a[kbp:])


def outer_dft(w, x, tc=1024):
    r2, k = w.shape
    inner, _, ch = x.shape
    packed_in = x.dtype == U32
    mult = 2 if packed_in else 1
    return pl.pallas_call(
        functools.partial(_outer_dft_kernel, packed_in=packed_in),
        grid=(inner // INNER_STEP, ch // tc),
        in_specs=[pl.BlockSpec((r2, k), lambda i, j: (0, 0)),
                  pl.BlockSpec((INNER_STEP, k, tc), lambda i, j: (i, 0, j))],
        out_specs=pl.BlockSpec((r2 // 2, INNER_STEP, mult * tc), lambda i, j: (0, i, j)),
        out_shape=jax.ShapeDtypeStruct((r2 // 2, inner, mult * ch), U32),
        compiler_params=_cparams("parallel", "parallel"),
        name="hyena_outer_dft",
    )(w, x)


def _filter_spectrum_kernel(tf_ref, a_ref, inv_ref, o_ref):
    inv = inv_ref[...]
    for b in range(tf_ref.shape[0]):
        x = jnp.concatenate(_unpack_pair(a_ref[b]), axis=0)
        o_ref[b] = (_dot(tf_ref[b], x) * inv).astype(o_ref.dtype)


def filter_spectrum(tf, a, inv_l1, kb=4, tc=1024):
    kbp, inner, ch = a.shape
    r = 2 * inner
    return pl.pallas_call(
        _filter_spectrum_kernel,
        grid=(kbp // kb, ch // tc),
        in_specs=[pl.BlockSpec((kb, r, r), lambda i, j: (i, 0, 0)),
                  pl.BlockSpec((kb, inner, tc), lambda i, j: (i, 0, j)),
                  pl.BlockSpec((1, tc), lambda i, j: (0, j))],
        out_specs=pl.BlockSpec((kb, r, tc), lambda i, j: (i, 0, j)),
        out_shape=jax.ShapeDtypeStruct((kbp, r, ch), BF16),
        compiler_params=_cparams("parallel", "arbitrary"),
        name="hyena_filter_spectrum",
    )(tf, a, inv_l1)


def _spectral_conv_kernel(tf_ref, ti_ref, a_ref, k_ref, g_ref):
    half = a_ref.shape[1]
    for b in range(tf_ref.shape[0]):
        x = jnp.concatenate(_unpack_pair(a_ref[b]), axis=0)
        z = _dot(tf_ref[b], x)
        zr, zi = z[:half], z[half:]
        kr = k_ref[b, :half, :].astype(F32)
        ki = k_ref[b, half:, :].astype(F32)
        y = jnp.concatenate([zr * kr - zi * ki, zr * ki + zi * kr], axis=0).astype(BF16)
        g = _dot(ti_ref[b], y)
        g_ref[:, b, :] = _pack_pair(g[:half], g[half:])


def spectral_conv(tf, ti, a, kf, order, tc=512):
    kbp, inner, ch = a.shape
    r = 2 * inner
    kb = INNER_STEP
    ncb = ch // tc
    return pl.pallas_call(
        _spectral_conv_kernel,
        grid=(kbp // kb, ncb),
        in_specs=[pl.BlockSpec((kb, r, r), lambda i, j: (i, 0, 0)),
                  pl.BlockSpec((kb, r, r), lambda i, j: (i, 0, 0)),
                  pl.BlockSpec((kb, inner, tc), lambda i, j: (i, 0, j)),
                  pl.BlockSpec((kb, r, tc), lambda i, j: (i, 0, order * ncb + j))],
        out_specs=pl.BlockSpec((inner, kb, tc), lambda i, j: (0, i, j)),
        out_shape=jax.ShapeDtypeStruct((inner, kbp, ch), U32),
        compiler_params=_cparams("parallel", "arbitrary"),
        name="hyena_spectral_conv",
    )(tf, ti, a, kf)


def _outer_inverse_gate_kernel(wre_ref, wim_ref, g_ref, gate_ref, zp_ref, bias_ref, o_ref, *, token_major_out):
    wre = wre_ref[...]
    wim = wim_ref[...]
    bias = bias_ref[...]
    for j in range(g_ref.shape[0]):
        g_re, g_im = _unpack_pair(g_ref[j])
        z = gate_ref[j] * (_dot(wre, g_re) + _dot(wim, g_im) + bias * zp_ref[j])
        if token_major_out:
            o_ref[:, j, :] = z
        else:
            o_ref[j] = z


def outer_inverse_gate(wi_re, wi_im, g, gate, z_prev, bias, token_major_out, tc=512):
    r, kbp = wi_re.shape
    inner, _, ch = g.shape
    wspec = pl.BlockSpec((r, kbp), lambda i, j: (0, 0))
    zspec = pl.BlockSpec((INNER_STEP, r, tc), lambda i, j: (i, 0, j))
    if token_major_out:
        ospec, oshape = pl.BlockSpec((r, INNER_STEP, tc), lambda i, j: (0, i, j)), (r, inner, ch)
    else:
        ospec, oshape = zspec, (inner, r, ch)
    return pl.pallas_call(
        functools.partial(_outer_inverse_gate_kernel, token_major_out=token_major_out),
        grid=(inner // INNER_STEP, ch // tc),
        in_specs=[wspec, wspec, pl.BlockSpec((INNER_STEP, kbp, tc), lambda i, j: (i, 0, j)), zspec, zspec,
                  pl.BlockSpec((1, tc), lambda i, j: (0, j))],
        out_specs=ospec,
        out_shape=jax.ShapeDtypeStruct(oshape, F32),
        compiler_params=_cparams("parallel", "parallel"),
        name="hyena_outer_inverse_gate",
    )(wi_re, wi_im, g, gate, z_prev, bias)


def _short_conv_kernel(u_ref, prev_ref, next_ref, w_ref, v_ref, x1_ref, x2_ref, *, nb):
    i = pl.program_id(0)
    tr = u_ref.shape[0]
    ch = v_ref.shape[2]
    halo = prev_ref.shape[0]
    row = lax.broadcasted_iota(jnp.int32, (tr, ch), 0)
    has_prev = jnp.where(i > 0, 1.0, 0.0)
    has_next = jnp.where(i < nb - 1, 1.0, 0.0)
    for part, o_ref in enumerate((v_ref, x1_ref, x2_ref)):
        cols = slice(part * ch, (part + 1) * ch)
        x = u_ref[:, cols].astype(F32)
        prev_row = prev_ref[halo - 1:halo, cols].astype(F32) * has_prev
        next_row = next_ref[0:1, cols].astype(F32) * has_next
        before = jnp.where(row == 0, prev_row, pltpu.roll(x, 1, axis=0))
        after = jnp.where(row == tr - 1, next_row, pltpu.roll(x, tr - 1, axis=0))
        w = w_ref[:, cols]
        y = before * w[0:1] + x * w[1:2] + after * w[2:3]
        for j in range(tr // DFT_N2):
            o_ref[:, j, :] = y[j * DFT_N2:(j + 1) * DFT_N2]


def short_conv(proj, short_w):
    seq = proj.shape[0]
    width = 3 * HY_W
    tr = INNER_STEP * DFT_N2
    nb = seq // tr
    halo = BF16_TILE_ROWS
    per = tr // halo
    out = jax.ShapeDtypeStruct((DFT_N2, seq // DFT_N2, HY_W), F32)
    return pl.pallas_call(
        functools.partial(_short_conv_kernel, nb=nb),
        grid=(nb,),
        in_specs=[pl.BlockSpec((tr, width), lambda i: (i, 0)),
                  pl.BlockSpec((halo, width), lambda i: (jnp.maximum(i * per - 1, 0), 0)),
                  pl.BlockSpec((halo, width), lambda i: (jnp.minimum((i + 1) * per, nb * per - 1), 0)),
                  pl.BlockSpec((3, width), lambda i: (0, 0))],
        out_specs=[pl.BlockSpec((DFT_N2, INNER_STEP, HY_W), lambda i: (0, i, 0))] * 3,
        out_shape=[out, out, out],
        compiler_params=_cparams("parallel"),
        name="hyena_short_conv",
    )(proj, proj, proj, short_w)


def _filter_mlp_kernel(z_ref, w1_ref, b1_ref, w2_ref, b2_ref, o_ref):
    hdot = functools.partial(jnp.dot, preferred_element_type=F32, precision=HIGHEST)
    h = jnp.sin(hdot(z_ref[...], w1_ref[...]) + b1_ref[...])
    for i in range(w2_ref.shape[0]):
        h = jnp.sin(hdot(h, w2_ref[i]) + b2_ref[i])
    o_ref[...] = h


def filter_mlp(feat, w1p, b1, w2, b2, tr=1024):
    seq, fp = feat.shape
    wd = w1p.shape[1]
    ni = w2.shape[0]
    tr = min(tr, seq)
    return pl.pallas_call(
        _filter_mlp_kernel,
        grid=(seq // tr,),
        in_specs=[pl.BlockSpec((tr, fp), lambda i: (i, 0)),
                  pl.BlockSpec((fp, wd), lambda i: (0, 0)),
                  pl.BlockSpec((1, wd), lambda i: (0, 0)),
                  pl.BlockSpec((ni, wd, wd), lambda i: (0, 0, 0)),
                  pl.BlockSpec((ni, 1, wd), lambda i: (0, 0, 0))],
        out_specs=pl.BlockSpec((tr, wd), lambda i: (i, 0)),
        out_shape=jax.ShapeDtypeStruct((seq, wd), F32),
        compiler_params=_cparams("parallel"),
        name="hyena_filter_mlp",
    )(feat, w1p, b1.reshape(1, wd), w2, b2.reshape(ni, 1, wd))


def _filter_expand_kernel(h_ref, t_ref, w3_ref, dl_ref, o_ref, s_ref):
    @pl.when(pl.program_id(0) == 0)
    def _():
        s_ref[...] = jnp.zeros_like(s_ref)

    f = _dot(h_ref[...].astype(BF16), w3_ref[0].astype(BF16))
    f = f * jnp.exp(-t_ref[...] * dl_ref[...])
    tr, ch = f.shape
    words = _pack_pair(f[:, :ch // 2], f[:, ch // 2:])
    for j in range(tr // DFT_N2):
        o_ref[:, j, :] = words[j * DFT_N2:(j + 1) * DFT_N2]
    s_ref[...] += jnp.sum(jnp.abs(f).reshape(tr // 8, 8, ch), axis=0)


def filter_expand(h_full, t_full, w3_halves, deltas2):
    n, wd = h_full.shape
    ch = w3_halves.shape[2]
    tr = INNER_STEP * DFT_N2
    nb = n // tr
    return pl.pallas_call(
        _filter_expand_kernel,
        grid=(nb,),
        in_specs=[pl.BlockSpec((tr, wd), lambda i: (i, 0)),
                  pl.BlockSpec((tr, 1), lambda i: (i, 0)),
                  pl.BlockSpec((1, wd, ch), lambda i: (i // (nb // 2), 0, 0)),
                  pl.BlockSpec((1, ch), lambda i: (0, 0))],
        out_specs=[pl.BlockSpec((DFT_N2, INNER_STEP, ch // 2), lambda i: (0, i, 0)),
                   pl.BlockSpec((8, ch), lambda i: (0, 0))],
        out_shape=[jax.ShapeDtypeStruct((DFT_N2, n // DFT_N2, ch // 2), U32), jax.ShapeDtypeStruct((8, ch), F32)],
        compiler_params=_cparams("arbitrary"),
        name="hyena_filter_expand",
    )(h_full, t_full, w3_halves, deltas2)


def _filter_positions(seq_len):
    pos = np.arange(seq_len, dtype=np.float64)
    t = np.linspace(0.0, 1.0, seq_len)
    ang = (2.0 * math.pi / seq_len) * pos
    freqs = np.linspace(1e-4, HY_BANDS - 1, HY_BANDS)
    feat = np.concatenate([t[:, None], np.cos(ang[:, None] * freqs), -np.sin(ang[:, None] * freqs)], -1)
    feat = np.pad(feat, ((0, 0), (0, HY_FEAT_PAD - feat.shape[1])))
    t_full = np.concatenate([t, np.zeros(1), t[:0:-1]])[:, None]
    max_decay = math.log(HY_TARGET) / HY_SHORT_PCT
    min_decay = math.log(HY_TARGET) / HY_LONG_PCT
    deltas = np.abs(np.linspace(min_decay, max_decay, HY_W))
    return (jnp.asarray(feat, F32), jnp.asarray(t_full, F32),
            jnp.asarray(np.tile(deltas, 2)[None, :], F32))


def hyena_filter_spectra(seq_len, tabs, consts, w1, b1, w2, b2, w3):
    feat, t_full, deltas2 = consts
    w1p = jnp.pad(w1, ((0, HY_FEAT_PAD - w1.shape[0]), (0, 0)))
    h = filter_mlp(feat, w1p, b1, w2, b2)
    h_full = jnp.concatenate([h, jnp.zeros((1, h.shape[1]), F32), jnp.flip(h[1:], 0)], 0)
    w3r = w3.reshape(w3.shape[0], 2, 2, HY_W)
    w3_halves = jnp.transpose(w3r, (2, 0, 1, 3)).reshape(2, w3.shape[0], 2 * HY_W)
    full, sabs = filter_expand(h_full, t_full, w3_halves, deltas2)
    inv_l1 = 1.0 / jnp.sum(sabs, axis=0, keepdims=True)
    a = outer_dft(tabs["wf_filt"], full)
    return filter_spectrum(tabs["tf"], a, inv_l1)


def hyena_mixer(proj, tabs, kf, short_w, bias):
    seq = proj.shape[0]
    z, x1, x2 = short_conv(proj, short_w)
    for order, gate in enumerate((x1, x2)):
        a = outer_dft(tabs["wf_data"], z)
        g = spectral_conv(tabs["tf"], tabs["ti"], a, kf, order)
        z = outer_inverse_gate(tabs["wi_re"], tabs["wi_im"], g, gate, z,
                               bias[order].reshape(1, HY_W).astype(F32), token_major_out=(order == 1))
    return z.reshape(seq, HY_W)


def _chunk_rows(c, t):
    return pl.ds(pl.multiple_of(c * t, t), t)


def _retention_kernel(lg_ref, q_ref, k_ref, v_ref, g_ref, o_ref, rstore_ref, s_ref, *, nblk, cpb, t):
    h = pl.program_id(0)
    sweep = pl.program_id(1)
    i = pl.program_id(2)
    lgf = lg_ref[0, h]
    lgb = lg_ref[1, h]
    pos = lax.broadcasted_iota(jnp.int32, (t, 1), 0).astype(F32)
    chunk_len = jnp.full((1, RET_DV), float(t), F32)

    @pl.when(i == 0)
    def _():
        s_ref[...] = jnp.zeros_like(s_ref)

    @pl.when(sweep == 0)
    def _():
        def body(c, carry):
            cc = cpb - 1 - c
            rows = _chunk_rows(cc, t)
            rstore_ref[(nblk - 1 - i) * cpb + cc] = s_ref[...]
            kw = (k_ref[rows, :].astype(F32) * jnp.exp(lgb * pos)).astype(BF16)
            s_ref[...] = s_ref[...] * jnp.exp(lgb * chunk_len) + _dot_tn(kw, v_ref[rows, :])
            return carry

        lax.fori_loop(0, cpb, body, 0)

    @pl.when(sweep == 1)
    def _():
        ri = lax.broadcasted_iota(jnp.int32, (t, t), 0)
        ci = lax.broadcasted_iota(jnp.int32, (t, t), 1)
        diff = (ri - ci).astype(F32)
        decay = (jnp.where(diff >= 0, jnp.exp(lgf * jnp.maximum(diff, 0.0)), 0.0)
                 + jnp.where(diff <= 0, jnp.exp(lgb * jnp.maximum(-diff, 0.0)), 0.0))
        q_fwd = jnp.exp(lgf * (pos + 1.0)) * (RET_DK ** -0.5)
        q_bwd = jnp.exp(lgb * (t - pos)) * (RET_DK ** -0.5)
        k_fwd = jnp.exp(lgf * (t - 1.0 - pos))

        def body(c, carry):
            rows = _chunk_rows(c, t)
            q = q_ref[rows, :].astype(F32)
            k = k_ref[rows, :]
            v = v_ref[rows, :]
            scores = _dot_nt((q * (RET_DK ** -0.5)).astype(BF16), k) * decay
            y = _dot(scores.astype(BF16), v)
            y = y + _dot((q * q_fwd).astype(BF16), s_ref[...].astype(BF16))
            y = y + _dot((q * q_bwd).astype(BF16), rstore_ref[i * cpb + c].astype(BF16))
            kw = (k.astype(F32) * k_fwd).astype(BF16)
            s_ref[...] = s_ref[...] * jnp.exp(lgf * chunk_len) + _dot_tn(kw, v)
            y = y * lax.rsqrt(jnp.mean(y * y, axis=-1, keepdims=True) + EPS)
            gt = g_ref[rows, :].astype(F32)
            o_ref[rows, :] = (gt * jax.nn.sigmoid(gt) * y).astype(o_ref.dtype)
            return carry

        lax.fori_loop(0, cpb, body, 0)


CHUNKS_PER_STEP = 4


def retention_mixer(proj, col0, log_decay, t=CHUNK):
    seq = proj.shape[0]
    nc = seq // t
    cpb = math.gcd(CHUNKS_PER_STEP, nc)
    nblk = nc // cpb
    tb = cpb * t
    qb = col0 // RET_DK
    kb = qb + RET_H
    vb = (col0 + 2 * RET_H * RET_DK) // RET_DV
    gb = vb + RET_H

    def rows(sweep, i):
        return sweep * i + (1 - sweep) * (nblk - 1 - i)

    return pl.pallas_call(
        functools.partial(_retention_kernel, nblk=nblk, cpb=cpb, t=t),
        grid=(RET_H, 2, nblk),
        in_specs=[pl.BlockSpec(memory_space=pltpu.SMEM),
                  pl.BlockSpec((tb, RET_DK), lambda h, s, i: (s * i, qb + h)),
                  pl.BlockSpec((tb, RET_DK), lambda h, s, i: (rows(s, i), kb + h)),
                  pl.BlockSpec((tb, RET_DV), lambda h, s, i: (rows(s, i), vb + h)),
                  pl.BlockSpec((tb, RET_DV), lambda h, s, i: (s * i, gb + h))],
        out_specs=pl.BlockSpec((tb, RET_DV), lambda h, s, i: (s * i, h)),
        out_shape=jax.ShapeDtypeStruct((seq, RET_H * RET_DV), BF16),
        scratch_shapes=[pltpu.VMEM((nc, RET_DK, RET_DV), F32), pltpu.VMEM((RET_DK, RET_DV), F32)],
        compiler_params=_cparams("arbitrary", "arbitrary", "arbitrary"),
        name="retention",
    )(log_decay, proj, proj, proj, proj)


def _log_sigmoid(x):
    return jnp.minimum(x, 0.0) - jnp.log(1.0 + jnp.exp(-jnp.abs(x)))


def _mlstm_gates(gc_ref, gr_ref, bias_ref, h, direction, rows):
    bi = bias_ref[direction * 2 * ML_H + h]
    bf = bias_ref[direction * 2 * ML_H + ML_H + h]
    a = 2 * direction
    ig_c = gc_ref[0, rows, a:a + 1] + bi
    lf_c = _log_sigmoid(gc_ref[0, rows, a + 1:a + 2] + bf)
    ig_r = gr_ref[0, a:a + 1, rows] + bi
    lf_r = _log_sigmoid(gr_ref[0, a + 1:a + 2, rows] + bf)
    return ig_c, lf_c, ig_r, lf_r


def _split3(x):
    hi = x.astype(BF16)
    rest = x - hi.astype(F32)
    mid = rest.astype(BF16)
    return hi, mid, (rest - mid.astype(F32)).astype(BF16)


def _running_sums(lf_fwd, lf_bwd, as_rows):
    t = lf_fwd.shape[1] if as_rows else lf_fwd.shape[0]
    ri = lax.broadcasted_iota(jnp.int32, (t, t), 0)
    ci = lax.broadcasted_iota(jnp.int32, (t, t), 1)
    tri = (ri >= ci).astype(BF16)
    if as_rows:
        sel = lax.broadcasted_iota(jnp.int32, (8, t), 0)
        both = jnp.where(sel == 0, lf_fwd, jnp.where(sel == 1, lf_bwd, 0.0))
        left = sum(_dot_nt(p, tri) for p in _split3(both))
        left_f, left_b = left[0:1, :], left[1:2, :]
        total_b = jnp.sum(lf_bwd, axis=1, keepdims=True)
    else:
        sel = lax.broadcasted_iota(jnp.int32, (t, LANES), 1)
        both = jnp.where(sel == 0, lf_fwd, jnp.where(sel == 1, lf_bwd, 0.0))
        left = sum(_dot(tri, p) for p in _split3(both))
        left_f, left_b = left[:, 0:1], left[:, 1:2]
        total_b = jnp.sum(lf_bwd, axis=0, keepdims=True)
    return left_f, total_b - left_b + lf_bwd


def _mlstm_state_step(k, v, ig_c, cum_c, total, c_ref, n_ref, m_ref):
    a = total - cum_c + ig_c
    m_loc = jnp.max(a, axis=0, keepdims=True)
    kw = k * jnp.exp(a - m_loc)
    kv = _dot_tn(kw.astype(BF16), v)
    ksum = jnp.sum(kw, axis=0, keepdims=True)
    m_old = m_ref[0:1, 0:1]
    m_new = jnp.maximum(total + m_old, m_loc)
    sp = jnp.exp(total + m_old - m_new)
    sc = jnp.exp(m_loc - m_new)
    c_ref[...] = sp * c_ref[...] + sc * kv
    n_ref[...] = sp * n_ref[...] + sc * jnp.broadcast_to(ksum, n_ref.shape)
    m_ref[...] = jnp.broadcast_to(m_new, m_ref.shape)


def _mlstm_output(qk, q, v, ig_r, cum_c, cum_r, c_prev, n_prev, m_prev, backward):
    t = q.shape[0]
    ri = lax.broadcasted_iota(jnp.int32, (t, t), 0)
    ci = lax.broadcasted_iota(jnp.int32, (t, t), 1)
    keep = (ri <= ci) if backward else (ri >= ci)
    dlog = jnp.where(keep, cum_c - cum_r + ig_r, -jnp.inf)
    inter = cum_c + m_prev
    m_t = jnp.maximum(inter, jnp.max(dlog, axis=-1, keepdims=True))
    s = qk * jnp.exp(dlog - m_t)
    wi = jnp.exp(inter - m_t)
    num = _dot(s.astype(BF16), v) + wi * _dot(q.astype(BF16), c_prev.astype(BF16))
    den = jnp.sum(s, axis=-1, keepdims=True) + wi * jnp.sum(q * n_prev, axis=-1, keepdims=True)
    return num / jnp.maximum(jnp.abs(den), jnp.exp(-m_t))


def _mlstm_kernel(bias_ref, q_ref, k_ref, v_ref, o_ref, gc_ref, gr_ref, gain_ref, out_ref,
                  cstore_ref, nstore_ref, mstore_ref, c_ref, n_ref, m_ref, *, nblk, cpb, t):
    h = pl.program_id(0)
    sweep = pl.program_id(1)
    i = pl.program_id(2)

    @pl.when(i == 0)
    def _():
        c_ref[...] = jnp.zeros_like(c_ref)
        n_ref[...] = jnp.zeros_like(n_ref)
        m_ref[...] = jnp.zeros_like(m_ref)

    @pl.when(sweep == 0)
    def _():
        def body(c, carry):
            cc = cpb - 1 - c
            rows = _chunk_rows(cc, t)
            n = (nblk - 1 - i) * cpb + cc
            cstore_ref[n] = c_ref[...]
            nstore_ref[n] = n_ref[...]
            mstore_ref[n] = m_ref[...]
            k = k_ref[rows, :].astype(F32) * (ML_DK ** -0.5)
            _, lf_c, _, _ = _mlstm_gates(gc_ref, gr_ref, bias_ref, h, 0, rows)
            ig_c, lb_c, _, _ = _mlstm_gates(gc_ref, gr_ref, bias_ref, h, 1, rows)
            _, cumb_c = _running_sums(lf_c, lb_c, False)
            total = jnp.sum(lb_c, axis=0, keepdims=True)
            _mlstm_state_step(k, v_ref[rows, :], ig_c, cumb_c, total, c_ref, n_ref, m_ref)
            return carry

        lax.fori_loop(0, cpb, body, 0)

    @pl.when(sweep == 1)
    def _():
        def body(c, carry):
            rows = _chunk_rows(c, t)
            n = i * cpb + c
            k = k_ref[rows, :].astype(F32) * (ML_DK ** -0.5)
            v = v_ref[rows, :]
            q = q_ref[rows, :].astype(F32)
            qk = _dot_nt(q_ref[rows, :], k.astype(BF16))
            ig_c, lf_c, ig_r, lf_r = _mlstm_gates(gc_ref, gr_ref, bias_ref, h, 0, rows)
            _, lb_c, igb_r, lb_r = _mlstm_gates(gc_ref, gr_ref, bias_ref, h, 1, rows)
            cum_c, cumb_c = _running_sums(lf_c, lb_c, False)
            cum_r, cumb_r = _running_sums(lf_r, lb_r, True)
            hf = _mlstm_output(qk, q, v, ig_r, cum_c, cum_r, c_ref[...], n_ref[0:1, :], m_ref[0:1, 0:1], False)
            total = jnp.sum(lf_c, axis=0, keepdims=True)
            _mlstm_state_step(k, v, ig_c, cum_c, total, c_ref, n_ref, m_ref)
            hb = _mlstm_output(qk, q, v, igb_r, cumb_c, cumb_r, cstore_ref[n], nstore_ref[n][0:1, :],
                               mstore_ref[n][0:1, 0:1], True)
            y = hf + hb
            y = y * lax.rsqrt(jnp.mean(y * y, axis=-1, keepdims=True) + EPS) * gain_ref[...]
            out_ref[rows, :] = (jax.nn.sigmoid(o_ref[rows, :].astype(F32)) * y).astype(out_ref.dtype)
            return carry

        lax.fori_loop(0, cpb, body, 0)


def mlstm_mixer(proj, gates, col0, gate_bias, norm_gain, t=CHUNK):
    seq = proj.shape[0]
    nc = seq // t
    cpb = math.gcd(CHUNKS_PER_STEP, nc)
    nblk = nc // cpb
    tb = cpb * t
    qb = col0 // ML_DK
    kb = qb + ML_H
    vb = (col0 + 2 * ML_H * ML_DK) // ML_DV
    ob = vb + ML_H
    g = gates[:, :4 * ML_H].reshape(seq, 2, 2, ML_H)
    g = jnp.transpose(g, (3, 0, 1, 2)).reshape(ML_H, seq, 4)
    g_cols = g
    g_rows = jnp.transpose(g, (0, 2, 1))

    def rows(sweep, i):
        return sweep * i + (1 - sweep) * (nblk - 1 - i)

    return pl.pallas_call(
        functools.partial(_mlstm_kernel, nblk=nblk, cpb=cpb, t=t),
        grid=(ML_H, 2, nblk),
        in_specs=[pl.BlockSpec(memory_space=pltpu.SMEM),
                  pl.BlockSpec((tb, ML_DK), lambda h, s, i: (s * i, qb + h)),
                  pl.BlockSpec((tb, ML_DK), lambda h, s, i: (rows(s, i), kb + h)),
                  pl.BlockSpec((tb, ML_DV), lambda h, s, i: (rows(s, i), vb + h)),
                  pl.BlockSpec((tb, ML_DV), lambda h, s, i: (s * i, ob + h)),
                  pl.BlockSpec((1, tb, 4), lambda h, s, i: (h, rows(s, i), 0)),
                  pl.BlockSpec((1, 4, tb), lambda h, s, i: (h, 0, rows(s, i))),
                  pl.BlockSpec((1, ML_DV), lambda h, s, i: (0, h))],
        out_specs=pl.BlockSpec((tb, ML_DV), lambda h, s, i: (s * i, h)),
        out_shape=jax.ShapeDtypeStruct((seq, ML_H * ML_DV), BF16),
        scratch_shapes=[pltpu.VMEM((nc, ML_DK, ML_DV), F32), pltpu.VMEM((nc, 8, ML_DK), F32),
                        pltpu.VMEM((nc, 8, LANES), F32), pltpu.VMEM((ML_DK, ML_DV), F32),
                        pltpu.VMEM((8, ML_DK), F32), pltpu.VMEM((8, LANES), F32)],
        compiler_params=_cparams("arbitrary", "arbitrary", "arbitrary"),
        name="mlstm",
    )(gate_bias, proj, proj, proj, proj, g_cols, g_rows, norm_gain.reshape(1, ML_H * ML_DV))


PERM_ROWS = 256
CLASS_RUN = BF16_TILE_ROWS
ATT_SUB_ROWS = 128


def _group_permutation(dilation):
    run = CLASS_RUN * dilation
    new = jnp.arange(PERM_ROWS, dtype=jnp.int32)
    within = new % run
    src = (new // run) * run + (within % CLASS_RUN) * dilation + within // CLASS_RUN
    return (src[:, None] == jnp.arange(PERM_ROWS, dtype=jnp.int32)[None, :]).astype(BF16)


def _attention_prep_kernel(gain_ref, p4_ref, p16_ref, x_ref, o1_ref, o4_ref, o16_ref):
    part = pl.program_id(1)

    @pl.when(part < 2)
    def _():
        gain = gain_ref[0]
        for h in range(ATT_H):
            cols = slice(h * ATT_DH, (h + 1) * ATT_DH)
            x = x_ref[:, cols].astype(F32)
            o1_ref[:, cols] = (x * lax.rsqrt(jnp.mean(x * x, axis=-1, keepdims=True) + EPS) * gain).astype(BF16)

    @pl.when(part == 2)
    def _():
        o1_ref[...] = x_ref[...]

    for s in range(x_ref.shape[0] // PERM_ROWS):
        rows = slice(s * PERM_ROWS, (s + 1) * PERM_ROWS)
        x = o1_ref[rows, :]
        o4_ref[rows, :] = _dot(p4_ref[...], x).astype(BF16)
        o16_ref[rows, :] = _dot(p16_ref[...], x).astype(BF16)


def attention_prep(proj, col0, qk_gain, tr=512):
    seq = proj.shape[0]
    cb = col0 // W_GROUP
    gains = jnp.stack([qk_gain[0] * (ATT_DH ** -0.5), qk_gain[1], jnp.ones_like(qk_gain[0])]).reshape(3, 1, ATT_DH)
    out = jax.ShapeDtypeStruct((seq, 3 * W_GROUP), BF16)
    ospec = pl.BlockSpec((tr, W_GROUP), lambda i, j: (i, j))
    pspec = pl.BlockSpec((PERM_ROWS, PERM_ROWS), lambda i, j: (0, 0))
    return pl.pallas_call(
        _attention_prep_kernel,
        grid=(seq // tr, 3),
        in_specs=[pl.BlockSpec((1, 1, ATT_DH), lambda i, j: (j, 0, 0)), pspec, pspec,
                  pl.BlockSpec((tr, W_GROUP), lambda i, j: (i, cb + j))],
        out_specs=[ospec, ospec, ospec],
        out_shape=[out, out, out],
        compiler_params=_cparams("parallel", "arbitrary"),
        name="attention_prep",
    )(gains.astype(F32), _group_permutation(4), _group_permutation(16), proj)


def _band_attention_kernel(q_ref, kp_ref, kc_ref, kn_ref, vp_ref, vc_ref, vn_ref, o_ref, lse_ref,
                           *, dilation, nblk):
    i = pl.program_id(1)
    tq = q_ref.shape[0] * CLASS_RUN
    hs = ATT_HALF_STEPS
    sq = min(ATT_SUB_ROWS, tq)
    sk = sq + 2 * hs
    ri = lax.broadcasted_iota(jnp.int32, (sq, sk), 0)
    ci = lax.broadcasted_iota(jnp.int32, (sq, sk), 1)
    off = ci - hs - ri
    first_col = jnp.where(i > 0, 0, hs)
    end_col = jnp.where(i < nblk - 1, tq + 2 * hs, tq + hs)
    in_band = jnp.abs(off) <= hs
    dist = (jnp.abs(off) * dilation).astype(F32)
    valid = [in_band & (ci + s0 >= first_col) & (ci + s0 < end_col) for s0 in range(0, tq, sq)]
    lane = lax.broadcasted_iota(jnp.int32, (sq, LANES), 1)
    lse_all = [jnp.zeros((sq, LANES), F32) for _ in valid]

    def rows(ref, cols):
        x = ref[:, :, cols]
        return x.reshape(x.shape[0] * CLASS_RUN, x.shape[2])

    for h in range(ATT_H):
        cols = slice(h * ATT_DH, (h + 1) * ATT_DH)
        slope = 2.0 ** (-8.0 * (h + 1) / ATT_H)
        qq = rows(q_ref, cols)
        kk = jnp.concatenate([rows(kp_ref, cols), rows(kc_ref, cols), rows(kn_ref, cols)], axis=0)
        vv = jnp.concatenate([rows(vp_ref, cols), rows(vc_ref, cols), rows(vn_ref, cols)], axis=0)
        outs = []
        for b, s0 in enumerate(range(0, tq, sq)):
            s = _dot_nt(qq[s0:s0 + sq], kk[s0:s0 + sk]) - slope * dist
            s = jnp.where(valid[b], s, NEG)
            m = jnp.max(s, axis=-1, keepdims=True)
            p = jnp.exp(s - m)
            den = jnp.sum(p, axis=-1, keepdims=True)
            outs.append(_dot(p.astype(BF16), vv[s0:s0 + sk]) / den)
            lse_all[b] = jnp.where(lane == h, m + jnp.log(den), lse_all[b])
        o = jnp.concatenate(outs, axis=0)
        o_ref[:, :, cols] = o.astype(o_ref.dtype).reshape(tq // CLASS_RUN, CLASS_RUN, ATT_DH)
    lse_ref[...] = jnp.concatenate(lse_all, axis=0).reshape(tq // CLASS_RUN, CLASS_RUN, LANES)


def band_attention(qkv, dilation, tq=256):
    seq = qkv.shape[0]
    n = seq // dilation
    tq = min(tq, n)
    nblk = n // tq
    hs = ATT_HALF_STEPS
    runs = n // CLASS_RUN
    tr = tq // CLASS_RUN
    hr = hs // CLASS_RUN
    ratio = tq // hs
    last_halo = n // hs - 1
    view = qkv.reshape(runs, dilation, CLASS_RUN, 3 * W_GROUP)

    def cur(part):
        return pl.BlockSpec((tr, None, CLASS_RUN, W_GROUP), lambda r, i: (i, r, 0, part))

    def prev(part):
        return pl.BlockSpec((hr, None, CLASS_RUN, W_GROUP), lambda r, i: (jnp.maximum(i * ratio - 1, 0), r, 0, part))

    def nxt(part):
        return pl.BlockSpec((hr, None, CLASS_RUN, W_GROUP),
                            lambda r, i: (jnp.minimum((i + 1) * ratio, last_halo), r, 0, part))

    o, lse = pl.pallas_call(
        functools.partial(_band_attention_kernel, dilation=dilation, nblk=nblk),
        grid=(dilation, nblk),
        in_specs=[cur(0), prev(1), cur(1), nxt(1), prev(2), cur(2), nxt(2)],
        out_specs=[pl.BlockSpec((tr, None, CLASS_RUN, W_GROUP), lambda r, i: (i, r, 0, 0)),
                   pl.BlockSpec((tr, None, CLASS_RUN, LANES), lambda r, i: (i, r, 0, 0))],
        out_shape=[jax.ShapeDtypeStruct((runs, dilation, CLASS_RUN, W_GROUP), BF16),
                   jax.ShapeDtypeStruct((runs, dilation, CLASS_RUN, LANES), F32)],
        compiler_params=_cparams("parallel", "arbitrary"),
        name=f"band_attention_d{dilation}",
    )(view, view, view, view, view, view, view)
    return o.reshape(seq, W_GROUP), lse.reshape(seq, LANES)


def _merge_branches_kernel(q4_ref, q16_ref, o1_ref, o4_ref, o16_ref, l1_ref, l4_ref, l16_ref, out_ref):
    def ungroup(qt, x):
        return _dot(qt, x)

    def ungroup_f32(qt, x):
        hi = x.astype(BF16)
        lo = (x - hi.astype(F32)).astype(BF16)
        return _dot(qt, hi) + _dot(qt, lo)

    for s in range(o1_ref.shape[0] // PERM_ROWS):
        rows = slice(s * PERM_ROWS, (s + 1) * PERM_ROWS)
        q4, q16 = q4_ref[...], q16_ref[...]
        l1 = l1_ref[rows, :]
        l2 = ungroup_f32(q4, l4_ref[rows, :])
        l3 = ungroup_f32(q16, l16_ref[rows, :])
        m = jnp.maximum(jnp.maximum(l1, l2), l3)
        e1, e2, e3 = jnp.exp(l1 - m), jnp.exp(l2 - m), jnp.exp(l3 - m)
        inv = 1.0 / (e1 + e2 + e3)
        w1, w2, w3 = e1 * inv, e2 * inv, e3 * inv
        o2 = ungroup(q4, o4_ref[rows, :])
        o3 = ungroup(q16, o16_ref[rows, :])
        for h in range(ATT_H):
            cols = slice(h * ATT_DH, (h + 1) * ATT_DH)
            out_ref[rows, cols] = (w1[:, h:h + 1] * o1_ref[rows, cols].astype(F32)
                                   + w2[:, h:h + 1] * o2[:, cols]
                                   + w3[:, h:h + 1] * o3[:, cols]).astype(out_ref.dtype)


def merge_branches(outs, lses, tr=512):
    seq = outs[0].shape[0]
    ospec = pl.BlockSpec((tr, W_GROUP), lambda i: (i, 0))
    lspec = pl.BlockSpec((tr, LANES), lambda i: (i, 0))
    pspec = pl.BlockSpec((PERM_ROWS, PERM_ROWS), lambda i: (0, 0))
    return pl.pallas_call(
        _merge_branches_kernel,
        grid=(seq // tr,),
        in_specs=[pspec, pspec, ospec, ospec, ospec, lspec, lspec, lspec],
        out_specs=ospec,
        out_shape=jax.ShapeDtypeStruct((seq, W_GROUP), BF16),
        compiler_params=_cparams("parallel"),
        name="attention_merge",
    )(_group_permutation(4).T, _group_permutation(16).T, *outs, *lses)


def dilated_attention(proj, col0, qk_gain):
    grouped = attention_prep(proj, col0, qk_gain)
    outs, lses = [], []
    for d, qkv in zip(ATT_DILATIONS, grouped):
        o, lse = band_attention(qkv, d)
        outs.append(o)
        lses.append(lse)
    return merge_branches(outs, lses)


def kernel(x, c, ada_w, ada_b, ada_table, w_in, w_out, hy_short, hy_w1, hy_b1, hy_w2, hy_b2, hy_w3, hy_bias,
           ret_decay, att_qk_gain, ml_gate_bias, ml_norm_gain, ffn_w1, ffn_w3, ffn_w2):
    batch, seq, d_model = x.shape
    depth = w_in.shape[0]
    d_main = 12 * W_GROUP
    hidden = ffn_w1.shape[2]

    tabs = _dft_tables(seq)
    consts = _filter_positions(seq)
    mod_shared = ada_modulation(c, ada_w, ada_b)

    w_in_b = w_in.astype(BF16)
    w_gate_b = jnp.pad(w_in[:, :, d_main:], ((0, 0), (0, 0), (0, LANES - (w_in.shape[2] - d_main)))).astype(BF16)
    w_out_b = w_out.astype(BF16)
    w1_b = ffn_w1.astype(BF16)
    w3_b = ffn_w3.astype(BF16)
    w2_b = ffn_w2.astype(BF16)

    rows = x.reshape(batch * seq, d_model)
    outs = []
    for b in range(batch):
        xb = rows[b * seq:(b + 1) * seq]
        for l in range(depth):
            mod = (mod_shared[b:b + 1] + ada_table[l].reshape(1, -1)).reshape(6, d_model)
            sh1, sc1, g1, sh2, sc2, g2 = (mod[i:i + 1] for i in range(6))
            proj, gates = norm_proj(xb, 1.0 + sc1, sh1, w_in_b, l, d_main, w_gate_b[l])

            kf = hyena_filter_spectra(seq, tabs, consts, hy_w1[l], hy_b1[l], hy_w2[l], hy_b2[l], hy_w3[l])
            y_a = hyena_mixer(proj, tabs, kf, hy_short[l], hy_bias[l])
            y_b = retention_mixer(proj, 3 * W_GROUP, jax.nn.log_sigmoid(ret_decay[l].astype(F32)))
            y_c = dilated_attention(proj, 6 * W_GROUP, att_qk_gain[l])
            y_d = mlstm_mixer(proj, gates, 9 * W_GROUP, ml_gate_bias[l], ml_norm_gain[l])
            y = jnp.concatenate([y_a.astype(BF16), y_b, y_c, y_d], axis=-1)
            xb = mm_residual(y, w_out_b, l, xb, g1, tm=1024, tn=1024, tk=y.shape[1])

            u = norm_swiglu(xb, 1.0 + sc2, sh2, w1_b, w3_b, l)
            xb = mm_residual(u, w2_b, l, xb, g2, tm=1024, tn=512, tk=hidden // 2)
        outs.append(xb)
    return jnp.concatenate(outs, 0).reshape(batch, seq, d_model)
```

```python
import functools
import math

import numpy as np
import jax
import jax.numpy as jnp
from jax import lax
from jax.experimental import pallas as pl
from jax.experimental.pallas import tpu as pltpu

F32 = jnp.float32
BF16 = jnp.bfloat16
HIGHEST = lax.Precision.HIGHEST

EPS = 1e-6
NEG = -1e30

V7X_VMEM_LIMIT_BYTES = 56 * 1024 * 1024
LANES = 128
BF16_TILE_ROWS = 16

W_GROUP = 1024
RET_H, RET_DK, RET_DV = 4, 128, 256
ATT_H, ATT_DH = 8, 128
ATT_HALF_STEPS = 64
ATT_DILATIONS = (1, 4, 16)
ML_H, ML_DK, ML_DV = 4, 128, 256
HY_BANDS = 16
HY_W = W_GROUP
HY_FILTER_WIDTH = 64
HY_TARGET, HY_SHORT_PCT, HY_LONG_PCT = 1e-2, 0.3, 1.5
HY_FEAT_PAD = 40
DFT_N2 = 128
CHUNK = 256


def _cparams(*sem):
    return pltpu.CompilerParams(dimension_semantics=sem, vmem_limit_bytes=V7X_VMEM_LIMIT_BYTES)


def _dot(a, b):
    return jnp.dot(a, b, preferred_element_type=F32)


def _dot_nt(a, b):
    return lax.dot_general(a, b, (((1,), (1,)), ((), ())), preferred_element_type=F32)


def _dot_tn(a, b):
    return lax.dot_general(a, b, (((0,), (0,)), ((), ())), preferred_element_type=F32)


def _ada_kernel(c_ref, w_ref, b_ref, o_ref):
    c = c_ref[...]
    s = c * jax.nn.sigmoid(c)
    o_ref[...] = jnp.dot(s, w_ref[...], preferred_element_type=F32, precision=HIGHEST) + b_ref[...]


def ada_modulation(c, ada_w, ada_b):
    d, n = ada_w.shape
    tn = 512
    c8 = jnp.broadcast_to(c.reshape(1, d), (8, d))
    out = pl.pallas_call(
        _ada_kernel,
        grid=(n // tn,),
        in_specs=[pl.BlockSpec((8, d), lambda j: (0, 0)),
                  pl.BlockSpec((d, tn), lambda j: (0, j)),
                  pl.BlockSpec((1, tn), lambda j: (0, j))],
        out_specs=pl.BlockSpec((8, tn), lambda j: (0, j)),
        out_shape=jax.ShapeDtypeStruct((8, n), F32),
        compiler_params=_cparams("parallel"),
        name="ada_modulation",
    )(c8, ada_w, ada_b.reshape(1, n))
    return out[0:1]


NORM_ROWS = 16


def _normalise_into(x_ref, sc_ref, sh_ref, h_ref):
    tm = x_ref.shape[0]
    sc = sc_ref[...]
    sh = sh_ref[...]

    def body(r, carry):
        rows = pl.ds(pl.multiple_of(r * NORM_ROWS, NORM_ROWS), NORM_ROWS)
        x = x_ref[rows, :]
        ms = jnp.mean(x * x, axis=-1, keepdims=True)
        h_ref[rows, :] = (x * lax.rsqrt(ms + EPS) * sc + sh).astype(h_ref.dtype)
        return carry

    lax.fori_loop(0, tm // NORM_ROWS, body, 0, unroll=4)


def _norm_proj_kernel(x_ref, sc_ref, sh_ref, w_ref, wg_ref, o_ref, g_ref, h_ref):
    @pl.when(pl.program_id(1) == 0)
    def _():
        _normalise_into(x_ref, sc_ref, sh_ref, h_ref)
        g_ref[...] = _dot(h_ref[...], wg_ref[...])

    o_ref[...] = _dot(h_ref[...], w_ref[...]).astype(o_ref.dtype)


def norm_proj(x, scale1p, shift, w_stack, layer, n, wg, tm=1024, tn=1024):
    m, d = x.shape
    ng = wg.shape[1]
    return pl.pallas_call(
        _norm_proj_kernel,
        grid=(m // tm, n // tn),
        in_specs=[pl.BlockSpec((tm, d), lambda i, j: (i, 0), pipeline_mode=pl.Buffered(1)),
                  pl.BlockSpec((1, d), lambda i, j: (0, 0)),
                  pl.BlockSpec((1, d), lambda i, j: (0, 0)),
                  pl.BlockSpec((None, d, tn), lambda i, j: (layer, 0, j)),
                  pl.BlockSpec((d, ng), lambda i, j: (0, 0))],
        out_specs=[pl.BlockSpec((tm, tn), lambda i, j: (i, j)),
                   pl.BlockSpec((tm, ng), lambda i, j: (i, 0))],
        out_shape=[jax.ShapeDtypeStruct((m, n), BF16), jax.ShapeDtypeStruct((m, ng), F32)],
        scratch_shapes=[pltpu.VMEM((tm, d), BF16)],
        compiler_params=_cparams("parallel", "arbitrary"),
        name="norm_proj",
    )(x, scale1p, shift, w_stack, wg)


def _norm_swiglu_kernel(x_ref, sc_ref, sh_ref, w1_ref, w3_ref, *rest, n_cast):
    cast_src, o_ref, cast_dst, h_ref = rest[:n_cast], rest[n_cast], rest[n_cast + 1:2 * n_cast + 1], rest[-1]

    @pl.when(pl.program_id(1) == 0)
    def _():
        _normalise_into(x_ref, sc_ref, sh_ref, h_ref)

    h = h_ref[...]
    a = _dot(h, w1_ref[...])
    b = _dot(h, w3_ref[...])
    o_ref[...] = (a * jax.nn.sigmoid(a) * b).astype(o_ref.dtype)
    for src, dst in zip(cast_src, cast_dst):
        dst[...] = src[...].astype(dst.dtype)


def norm_swiglu(x, scale1p, shift, w1, w3, to_cast=(), tm=1024, tn=256):
    m, d = x.shape
    n = w1.shape[1]
    gi, gj = m // tm, n // tn
    cast_in_specs, cast_out_specs, cast_shapes = [], [], []
    for stack, layer, swap in to_cast:
        rows, cols = stack.shape[1:]
        br, bc = (rows // gj, cols // gi) if swap else (rows // gi, cols // gj)
        idx = (lambda i, j: (j, i)) if swap else (lambda i, j: (i, j))
        cast_in_specs.append(pl.BlockSpec((None, br, bc), lambda i, j, idx=idx, layer=layer: (layer,) + idx(i, j)))
        cast_out_specs.append(pl.BlockSpec((br, bc), idx))
        cast_shapes.append(jax.ShapeDtypeStruct((rows, cols), BF16))
    res = pl.pallas_call(
        functools.partial(_norm_swiglu_kernel, n_cast=len(to_cast)),
        grid=(gi, gj),
        in_specs=[pl.BlockSpec((tm, d), lambda i, j: (i, 0)),
                  pl.BlockSpec((1, d), lambda i, j: (0, 0)),
                  pl.BlockSpec((1, d), lambda i, j: (0, 0)),
                  pl.BlockSpec((d, tn), lambda i, j: (0, j)),
                  pl.BlockSpec((d, tn), lambda i, j: (0, j))] + cast_in_specs,
        out_specs=[pl.BlockSpec((tm, tn), lambda i, j: (i, j))] + cast_out_specs,
        out_shape=[jax.ShapeDtypeStruct((m, n), BF16)] + cast_shapes,
        scratch_shapes=[pltpu.VMEM((tm, d), BF16)],
        compiler_params=_cparams("parallel", "arbitrary"),
        name="norm_swiglu",
    )(x, scale1p, shift, w1, w3, *[stack for stack, _, _ in to_cast])
    return res[0], res[1:]


def _mm_residual_kernel(y_ref, w_ref, x_ref, g_ref, o_ref):
    part = g_ref[...] * _dot(y_ref[...], w_ref[...])

    @pl.when(pl.program_id(2) == 0)
    def _():
        o_ref[...] = x_ref[...] + part

    @pl.when(pl.program_id(2) > 0)
    def _():
        o_ref[...] += part


def _mm_residual_1k_kernel(y_ref, w_ref, x_ref, g_ref, o_ref):
    o_ref[...] = x_ref[...] + g_ref[...] * _dot(y_ref[...], w_ref[...])


def mm_residual(y, w_stack, layer, x, gate, tm, tn, tk):
    m, kk = y.shape
    n = w_stack.shape[2]
    nk = kk // tk
    if nk == 1:
        return pl.pallas_call(
            _mm_residual_1k_kernel,
            grid=(m // tm, n // tn),
            in_specs=[pl.BlockSpec((tm, kk), lambda i, j: (i, 0)),
                      pl.BlockSpec((None, kk, tn), lambda i, j: (layer, 0, j)),
                      pl.BlockSpec((tm, tn), lambda i, j: (i, j)),
                      pl.BlockSpec((1, tn), lambda i, j: (0, j))],
            out_specs=pl.BlockSpec((tm, tn), lambda i, j: (i, j)),
            out_shape=jax.ShapeDtypeStruct((m, n), F32),
            compiler_params=_cparams("parallel", "arbitrary"),
            name="mm_residual",
        )(y, w_stack, x, gate)
    return pl.pallas_call(
        _mm_residual_kernel,
        grid=(m // tm, n // tn, nk),
        in_specs=[pl.BlockSpec((tm, tk), lambda i, j, k: (i, k)),
                  pl.BlockSpec((None, tk, tn), lambda i, j, k: (layer, k, j)),
                  pl.BlockSpec((tm, tn), lambda i, j, k: (i, j)),
                  pl.BlockSpec((1, tn), lambda i, j, k: (0, j))],
        out_specs=pl.BlockSpec((tm, tn), lambda i, j, k: (i, j)),
        out_shape=jax.ShapeDtypeStruct((m, n), F32),
        compiler_params=_cparams("parallel", "arbitrary", "arbitrary"),
        name="mm_residual_ksplit",
    )(y, w_stack, x, gate)


def _dft_tables(seq_len):
    n = 2 * seq_len
    n2 = DFT_N2
    n1 = n // n2
    half = n1 // 2
    kb = half + 1
    kbp = -(-kb // 8) * 8
    k1 = jnp.arange(kbp, dtype=jnp.int32)
    live = (k1 <= half)
    col = jnp.arange(n1, dtype=jnp.int32)
    ph = (2.0 * math.pi / n1) * ((k1[:, None] * col[None, :]) % n1).astype(F32)
    wf = jnp.stack([jnp.cos(ph), -jnp.sin(ph)], axis=0) * live[None, :, None]
    wf = wf.reshape(2 * kbp, n1)
    a = jnp.arange(n2, dtype=jnp.int32)
    mm = (a[None, None, :] * (k1[:, None, None] + n1 * a[None, :, None])) % n
    th = (2.0 * math.pi / n) * mm.astype(F32)
    c, s = jnp.cos(th), jnp.sin(th)
    tf = jnp.concatenate([jnp.concatenate([c, s], 2), jnp.concatenate([-s, c], 2)], 1)
    tf = tf * live[:, None, None]
    ti = jnp.transpose(tf, (0, 2, 1))
    wt = jnp.where((k1 == 0) | (k1 == half), 1.0, 2.0) * live / n
    row = jnp.arange(half, dtype=jnp.int32)
    ph2 = (2.0 * math.pi / n1) * ((row[:, None] * k1[None, :]) % n1).astype(F32)
    return dict(wf_data=wf[:, :half].astype(BF16), wf_filt=wf.astype(BF16), tf=tf.astype(BF16),
                ti=ti.astype(BF16), wi_re=(wt * jnp.cos(ph2)).astype(BF16), wi_im=(-wt * jnp.sin(ph2)).astype(BF16),
                kbp=kbp, n1=n1)


INNER_STEP = 8


U32 = jnp.uint32


def _pack_pair(hi, lo):
    hb = lax.bitcast_convert_type(hi.astype(BF16).astype(F32), U32)
    lb = lax.bitcast_convert_type(lo.astype(BF16).astype(F32), U32)
    return hb | (lb >> 16)


def _unpack_pair(w):
    hi = lax.bitcast_convert_type(w & jnp.uint32(0xFFFF0000), F32).astype(BF16)
    lo = lax.bitcast_convert_type(w << 16, F32).astype(BF16)
    return hi, lo


def _outer_dft_kernel(w_ref, x_ref, o_ref, *, packed_in):
    kbp = o_ref.shape[0]
    w = w_ref[...]
    for j in range(x_ref.shape[0]):
        if packed_in:
            parts = _unpack_pair(x_ref[j])
        else:
            parts = (x_ref[j].astype(BF16),)
        tc = parts[0].shape[1]
        for p, x in enumerate(parts):
            a = _dot(w, x)
            o_ref[:, j, p * tc:(p + 1) * tc] = _pack_pair(a[:kbp], a[kbp:])


def outer_dft(w, x, tc=1024):
    r2, k = w.shape
    inner, _, ch = x.shape
    packed_in = x.dtype == U32
    mult = 2 if packed_in else 1
    return pl.pallas_call(
        functools.partial(_outer_dft_kernel, packed_in=packed_in),
        grid=(inner // INNER_STEP, ch // tc),
        in_specs=[pl.BlockSpec((r2, k), lambda i, j: (0, 0)),
                  pl.BlockSpec((INNER_STEP, k, tc), lambda i, j: (i, 0, j))],
        out_specs=pl.BlockSpec((r2 // 2, INNER_STEP, mult * tc), lambda i, j: (0, i, j)),
        out_shape=jax.ShapeDtypeStruct((r2 // 2, inner, mult * ch), U32),
        compiler_params=_cparams("parallel", "parallel"),
        name="hyena_outer_dft",
    )(w, x)


def _filter_spectrum_kernel(tf_ref, a_ref, inv_ref, o_ref):
    inv = inv_ref[...]
    for b in range(tf_ref.shape[0]):
        x = jnp.concatenate(_unpack_pair(a_ref[b]), axis=0)
        o_ref[b] = (_dot(tf_ref[b], x) * inv).astype(o_ref.dtype)


def filter_spectrum(tf, a, inv_l1, kb=4, tc=1024):
    kbp, inner, ch = a.shape
    r = 2 * inner
    return pl.pallas_call(
        _filter_spectrum_kernel,
        grid=(kbp // kb, ch // tc),
        in_specs=[pl.BlockSpec((kb, r, r), lambda i, j: (i, 0, 0)),
                  pl.BlockSpec((kb, inner, tc), lambda i, j: (i, 0, j)),
                  pl.BlockSpec((1, tc), lambda i, j: (0, j))],
        out_specs=pl.BlockSpec((kb, r, tc), lambda i, j: (i, 0, j)),
        out_shape=jax.ShapeDtypeStruct((kbp, r, ch), BF16),
        compiler_params=_cparams("parallel", "arbitrary"),
        name="hyena_filter_spectrum",
    )(tf, a, inv_l1)


def _spectral_conv_kernel(tf_ref, ti_ref, a_ref, k_ref, g_ref):
    half = a_ref.shape[1]
    for b in range(tf_ref.shape[0]):
        x = jnp.concatenate(_unpack_pair(a_ref[b]), axis=0)
        z = _dot(tf_ref[b], x)
        zr, zi = z[:half], z[half:]
        kr = k_ref[b, :half, :].astype(F32)
        ki = k_ref[b, half:, :].astype(F32)
        y = jnp.concatenate([zr * kr - zi * ki, zr * ki + zi * kr], axis=0).astype(BF16)
        g = _dot(ti_ref[b], y)
        g_ref[:, b, :] = _pack_pair(g[:half], g[half:])


def spectral_conv(tf, ti, a, kf, order, tc=512):
    kbp, inner, ch = a.shape
    r = 2 * inner
    kb = INNER_STEP
    ncb = ch // tc
    return pl.pallas_call(
        _spectral_conv_kernel,
        grid=(kbp // kb, ncb),
        in_specs=[pl.BlockSpec((kb, r, r), lambda i, j: (i, 0, 0)),
                  pl.BlockSpec((kb, r, r), lambda i, j: (i, 0, 0)),
                  pl.BlockSpec((kb, inner, tc), lambda i, j: (i, 0, j)),
                  pl.BlockSpec((kb, r, tc), lambda i, j: (i, 0, order * ncb + j))],
        out_specs=pl.BlockSpec((inner, kb, tc), lambda i, j: (0, i, j)),
        out_shape=jax.ShapeDtypeStruct((inner, kbp, ch), U32),
        compiler_params=_cparams("parallel", "arbitrary"),
        name="hyena_spectral_conv",
    )(tf, ti, a, kf)


def _outer_inverse_gate_kernel(wre_ref, wim_ref, g_ref, gate_ref, zp_ref, bias_ref, o_ref, *, token_major_out):
    wre = wre_ref[...]
    wim = wim_ref[...]
    bias = bias_ref[...]
    for j in range(g_ref.shape[0]):
        g_re, g_im = _unpack_pair(g_ref[j])
        z = gate_ref[j] * (_dot(wre, g_re) + _dot(wim, g_im) + bias * zp_ref[j])
        if token_major_out:
            o_ref[:, j, :] = z
        else:
            o_ref[j] = z


def outer_inverse_gate(wi_re, wi_im, g, gate, z_prev, bias, token_major_out, tc=512):
    r, kbp = wi_re.shape
    inner, _, ch = g.shape
    wspec = pl.BlockSpec((r, kbp), lambda i, j: (0, 0))
    zspec = pl.BlockSpec((INNER_STEP, r, tc), lambda i, j: (i, 0, j))
    if token_major_out:
        ospec, oshape = pl.BlockSpec((r, INNER_STEP, tc), lambda i, j: (0, i, j)), (r, inner, ch)
    else:
        ospec, oshape = zspec, (inner, r, ch)
    return pl.pallas_call(
        functools.partial(_outer_inverse_gate_kernel, token_major_out=token_major_out),
        grid=(inner // INNER_STEP, ch // tc),
        in_specs=[wspec, wspec, pl.BlockSpec((INNER_STEP, kbp, tc), lambda i, j: (i, 0, j)), zspec, zspec,
                  pl.BlockSpec((1, tc), lambda i, j: (0, j))],
        out_specs=ospec,
        out_shape=jax.ShapeDtypeStruct(oshape, F32),
        compiler_params=_cparams("parallel", "parallel"),
        name="hyena_outer_inverse_gate",
    )(wi_re, wi_im, g, gate, z_prev, bias)


def _short_conv_kernel(u_ref, prev_ref, next_ref, w_ref, v_ref, x1_ref, x2_ref, *, nb):
    i = pl.program_id(0)
    tr = u_ref.shape[0]
    ch = v_ref.shape[2]
    halo = prev_ref.shape[0]
    row = lax.broadcasted_iota(jnp.int32, (tr, ch), 0)
    has_prev = jnp.where(i > 0, 1.0, 0.0)
    has_next = jnp.where(i < nb - 1, 1.0, 0.0)
    for part, o_ref in enumerate((v_ref, x1_ref, x2_ref)):
        cols = slice(part * ch, (part + 1) * ch)
        x = u_ref[:, cols].astype(F32)
        prev_row = prev_ref[halo - 1:halo, cols].astype(F32) * has_prev
        next_row = next_ref[0:1, cols].astype(F32) * has_next
        before = jnp.where(row == 0, prev_row, pltpu.roll(x, 1, axis=0))
        after = jnp.where(row == tr - 1, next_row, pltpu.roll(x, tr - 1, axis=0))
        w = w_ref[:, cols]
        y = before * w[0:1] + x * w[1:2] + after * w[2:3]
        for j in range(tr // DFT_N2):
            o_ref[:, j, :] = y[j * DFT_N2:(j + 1) * DFT_N2]


def short_conv(proj, short_w):
    seq = proj.shape[0]
    width = 3 * HY_W
    tr = INNER_STEP * DFT_N2
    nb = seq // tr
    halo = BF16_TILE_ROWS
    per = tr // halo
    out = jax.ShapeDtypeStruct((DFT_N2, seq // DFT_N2, HY_W), F32)
    return pl.pallas_call(
        functools.partial(_short_conv_kernel, nb=nb),
        grid=(nb,),
        in_specs=[pl.BlockSpec((tr, width), lambda i: (i, 0)),
                  pl.BlockSpec((halo, width), lambda i: (jnp.maximum(i * per - 1, 0), 0)),
                  pl.BlockSpec((halo, width), lambda i: (jnp.minimum((i + 1) * per, nb * per - 1), 0)),
                  pl.BlockSpec((3, width), lambda i: (0, 0))],
        out_specs=[pl.BlockSpec((DFT_N2, INNER_STEP, HY_W), lambda i: (0, i, 0))] * 3,
        out_shape=[out, out, out],
        compiler_params=_cparams("parallel"),
        name="hyena_short_conv",
    )(proj, proj, proj, short_w)


def _filter_mlp_kernel(z_ref, w1_ref, b1_ref, w2_ref, b2_ref, o_ref):
    hdot = functools.partial(jnp.dot, preferred_element_type=F32, precision=HIGHEST)
    h = jnp.sin(hdot(w1_ref[...], z_ref[...]) + b1_ref[...])
    for i in range(w2_ref.shape[0]):
        h = jnp.sin(hdot(w2_ref[i], h) + b2_ref[i])
    o_ref[...] = h


def filter_mlp(feat_t, w1t, b1, w2t, b2, tc=2048):
    fp, npos = feat_t.shape
    wd = w1t.shape[0]
    ni = w2t.shape[0]
    tc = min(tc, npos)
    return pl.pallas_call(
        _filter_mlp_kernel,
        grid=(npos // tc,),
        in_specs=[pl.BlockSpec((fp, tc), lambda i: (0, i)),
                  pl.BlockSpec((wd, fp), lambda i: (0, 0)),
                  pl.BlockSpec((wd, 1), lambda i: (0, 0)),
                  pl.BlockSpec((ni, wd, wd), lambda i: (0, 0, 0)),
                  pl.BlockSpec((ni, wd, 1), lambda i: (0, 0, 0))],
        out_specs=pl.BlockSpec((wd, tc), lambda i: (0, i)),
        out_shape=jax.ShapeDtypeStruct((wd, npos), F32),
        compiler_params=_cparams("parallel"),
        name="hyena_filter_mlp",
    )(feat_t, w1t, b1.reshape(wd, 1), w2t, b2.reshape(ni, wd, 1))


def _filter_expand_kernel(h_ref, t_ref, w3_ref, dl_ref, o_ref, s_ref):
    @pl.when(pl.program_id(0) == 0)
    def _():
        s_ref[...] = jnp.zeros_like(s_ref)

    f = _dot_tn(h_ref[...].astype(BF16), w3_ref[0].astype(BF16))
    f = f * (jnp.exp(-t_ref[:, 0:1] * dl_ref[...]) * t_ref[:, 1:2])
    tr, ch = f.shape
    words = _pack_pair(f[:, :ch // 2], f[:, ch // 2:])
    for j in range(tr // DFT_N2):
        o_ref[:, j, :] = words[j * DFT_N2:(j + 1) * DFT_N2]
    s_ref[...] += jnp.sum(jnp.abs(f).reshape(tr // 8, 8, ch), axis=0)


def filter_expand(h_t, t_mask, w3_halves, deltas2):
    wd, n = h_t.shape
    ch = w3_halves.shape[2]
    tr = INNER_STEP * DFT_N2
    nb = n // tr
    return pl.pallas_call(
        _filter_expand_kernel,
        grid=(nb,),
        in_specs=[pl.BlockSpec((wd, tr), lambda i: (0, i)),
                  pl.BlockSpec((tr, 2), lambda i: (i, 0)),
                  pl.BlockSpec((1, wd, ch), lambda i: (i // (nb // 2), 0, 0)),
                  pl.BlockSpec((1, ch), lambda i: (0, 0))],
        out_specs=[pl.BlockSpec((DFT_N2, INNER_STEP, ch // 2), lambda i: (0, i, 0)),
                   pl.BlockSpec((8, ch), lambda i: (0, 0))],
        out_shape=[jax.ShapeDtypeStruct((DFT_N2, n // DFT_N2, ch // 2), U32), jax.ShapeDtypeStruct((8, ch), F32)],
        compiler_params=_cparams("arbitrary"),
        name="hyena_filter_expand",
    )(h_t, t_mask, w3_halves, deltas2)


def _filter_positions(seq_len):
    pos = np.arange(seq_len, dtype=np.float64)
    t = np.linspace(0.0, 1.0, seq_len)
    ang = (2.0 * math.pi / seq_len) * pos
    freqs = np.linspace(1e-4, HY_BANDS - 1, HY_BANDS)
    feat = np.concatenate([t[:, None], np.cos(ang[:, None] * freqs), -np.sin(ang[:, None] * freqs)], -1)
    feat = np.pad(feat, ((0, 0), (0, HY_FEAT_PAD - feat.shape[1])))
    tau = np.concatenate([np.arange(seq_len), np.zeros(1, np.int64), np.arange(seq_len - 1, 0, -1)])
    mask = np.ones(2 * seq_len)
    mask[seq_len] = 0.0
    t_mask = np.stack([t[tau], mask], axis=1)
    max_decay = math.log(HY_TARGET) / HY_SHORT_PCT
    min_decay = math.log(HY_TARGET) / HY_LONG_PCT
    deltas = np.abs(np.linspace(min_decay, max_decay, HY_W))
    return (jnp.asarray(feat[tau].T, F32), jnp.asarray(t_mask, F32),
            jnp.asarray(np.tile(deltas, 2)[None, :], F32))


def hyena_filter_spectra(seq_len, tabs, consts, w1, b1, w2, b2, w3):
    feat_t, t_mask, deltas2 = consts
    w1t = jnp.pad(w1, ((0, HY_FEAT_PAD - w1.shape[0]), (0, 0))).T
    h_t = filter_mlp(feat_t, w1t, b1, jnp.transpose(w2, (0, 2, 1)), b2)
    w3r = w3.reshape(w3.shape[0], 2, 2, HY_W)
    w3_halves = jnp.transpose(w3r, (2, 0, 1, 3)).reshape(2, w3.shape[0], 2 * HY_W)
    full, sabs = filter_expand(h_t, t_mask, w3_halves, deltas2)
    inv_l1 = 1.0 / jnp.sum(sabs, axis=0, keepdims=True)
    a = outer_dft(tabs["wf_filt"], full)
    return filter_spectrum(tabs["tf"], a, inv_l1)


def hyena_mixer(proj, tabs, kf, short_w, bias):
    seq = proj.shape[0]
    z, x1, x2 = short_conv(proj, short_w)
    for order, gate in enumerate((x1, x2)):
        a = outer_dft(tabs["wf_data"], z)
        g = spectral_conv(tabs["tf"], tabs["ti"], a, kf, order)
        z = outer_inverse_gate(tabs["wi_re"], tabs["wi_im"], g, gate, z,
                               bias[order].reshape(1, HY_W).astype(F32), token_major_out=(order == 1))
    return z.reshape(seq, HY_W)


def _chunk_rows(c, t):
    return pl.ds(pl.multiple_of(c * t, t), t)


def _retention_kernel(lg_ref, q_ref, k_ref, v_ref, g_ref, o_ref, rstore_ref, s_ref, *, nblk, cpb, t):
    h = pl.program_id(0)
    sweep = pl.program_id(1)
    i = pl.program_id(2)
    lgf = lg_ref[0, h]
    lgb = lg_ref[1, h]
    pos = lax.broadcasted_iota(jnp.int32, (t, 1), 0).astype(F32)
    chunk_len = jnp.full((1, RET_DV), float(t), F32)

    @pl.when(i == 0)
    def _():
        s_ref[...] = jnp.zeros_like(s_ref)

    @pl.when(sweep == 0)
    def _():
        def body(c, carry):
            cc = cpb - 1 - c
            rows = _chunk_rows(cc, t)
            rstore_ref[(nblk - 1 - i) * cpb + cc] = s_ref[...]
            kw = (k_ref[rows, :].astype(F32) * jnp.exp(lgb * pos)).astype(BF16)
            s_ref[...] = s_ref[...] * jnp.exp(lgb * chunk_len) + _dot_tn(kw, v_ref[rows, :])
            return carry

        lax.fori_loop(0, cpb, body, 0)

    @pl.when(sweep == 1)
    def _():
        ri = lax.broadcasted_iota(jnp.int32, (t, t), 0)
        ci = lax.broadcasted_iota(jnp.int32, (t, t), 1)
        diff = (ri - ci).astype(F32)
        decay = (jnp.where(diff >= 0, jnp.exp(lgf * jnp.maximum(diff, 0.0)), 0.0)
                 + jnp.where(diff <= 0, jnp.exp(lgb * jnp.maximum(-diff, 0.0)), 0.0))
        q_fwd = jnp.exp(lgf * (pos + 1.0)) * (RET_DK ** -0.5)
        q_bwd = jnp.exp(lgb * (t - pos)) * (RET_DK ** -0.5)
        k_fwd = jnp.exp(lgf * (t - 1.0 - pos))

        def body(c, carry):
            rows = _chunk_rows(c, t)
            q = q_ref[rows, :].astype(F32)
            k = k_ref[rows, :]
            v = v_ref[rows, :]
            scores = _dot_nt((q * (RET_DK ** -0.5)).astype(BF16), k) * decay
            y = _dot(scores.astype(BF16), v)
            y = y + _dot((q * q_fwd).astype(BF16), s_ref[...].astype(BF16))
            y = y + _dot((q * q_bwd).astype(BF16), rstore_ref[i * cpb + c].astype(BF16))
            kw = (k.astype(F32) * k_fwd).astype(BF16)
            s_ref[...] = s_ref[...] * jnp.exp(lgf * chunk_len) + _dot_tn(kw, v)
            y = y * lax.rsqrt(jnp.mean(y * y, axis=-1, keepdims=True) + EPS)
            gt = g_ref[rows, :].astype(F32)
            o_ref[rows, :] = (gt * jax.nn.sigmoid(gt) * y).astype(o_ref.dtype)
            return carry

        lax.fori_loop(0, cpb, body, 0)


CHUNKS_PER_STEP = 4


def retention_mixer(proj, col0, log_decay, t=CHUNK):
    seq = proj.shape[0]
    nc = seq // t
    cpb = math.gcd(CHUNKS_PER_STEP, nc)
    nblk = nc // cpb
    tb = cpb * t
    qb = col0 // RET_DK
    kb = qb + RET_H
    vb = (col0 + 2 * RET_H * RET_DK) // RET_DV
    gb = vb + RET_H

    def rows(sweep, i):
        return sweep * i + (1 - sweep) * (nblk - 1 - i)

    return pl.pallas_call(
        functools.partial(_retention_kernel, nblk=nblk, cpb=cpb, t=t),
        grid=(RET_H, 2, nblk),
        in_specs=[pl.BlockSpec(memory_space=pltpu.SMEM),
                  pl.BlockSpec((tb, RET_DK), lambda h, s, i: (s * i, qb + h)),
                  pl.BlockSpec((tb, RET_DK), lambda h, s, i: (rows(s, i), kb + h)),
                  pl.BlockSpec((tb, RET_DV), lambda h, s, i: (rows(s, i), vb + h)),
                  pl.BlockSpec((tb, RET_DV), lambda h, s, i: (s * i, gb + h))],
        out_specs=pl.BlockSpec((tb, RET_DV), lambda h, s, i: (s * i, h)),
        out_shape=jax.ShapeDtypeStruct((seq, RET_H * RET_DV), BF16),
        scratch_shapes=[pltpu.VMEM((nc, RET_DK, RET_DV), F32), pltpu.VMEM((RET_DK, RET_DV), F32)],
        compiler_params=_cparams("arbitrary", "arbitrary", "arbitrary"),
        name="retention",
    )(log_decay, proj, proj, proj, proj)


def _log_sigmoid(x):
    return jnp.minimum(x, 0.0) - jnp.log(1.0 + jnp.exp(-jnp.abs(x)))


def _mlstm_gates(gc_ref, gr_ref, bias_ref, h, direction, rows):
    bi = bias_ref[direction * 2 * ML_H + h]
    bf = bias_ref[direction * 2 * ML_H + ML_H + h]
    a = 2 * direction
    ig_c = gc_ref[0, rows, a:a + 1] + bi
    lf_c = _log_sigmoid(gc_ref[0, rows, a + 1:a + 2] + bf)
    ig_r = gr_ref[0, a:a + 1, rows] + bi
    lf_r = _log_sigmoid(gr_ref[0, a + 1:a + 2, rows] + bf)
    return ig_c, lf_c, ig_r, lf_r


def _split3(x):
    hi = x.astype(BF16)
    rest = x - hi.astype(F32)
    mid = rest.astype(BF16)
    return hi, mid, (rest - mid.astype(F32)).astype(BF16)


def _running_sums(lf_fwd, lf_bwd, as_rows):
    t = lf_fwd.shape[1] if as_rows else lf_fwd.shape[0]
    ri = lax.broadcasted_iota(jnp.int32, (t, t), 0)
    ci = lax.broadcasted_iota(jnp.int32, (t, t), 1)
    tri = (ri >= ci).astype(BF16)
    if as_rows:
        sel = lax.broadcasted_iota(jnp.int32, (8, t), 0)
        both = jnp.where(sel == 0, lf_fwd, jnp.where(sel == 1, lf_bwd, 0.0))
        left = sum(_dot_nt(p, tri) for p in _split3(both))
        left_f, left_b = left[0:1, :], left[1:2, :]
        total_b = jnp.sum(lf_bwd, axis=1, keepdims=True)
    else:
        sel = lax.broadcasted_iota(jnp.int32, (t, LANES), 1)
        both = jnp.where(sel == 0, lf_fwd, jnp.where(sel == 1, lf_bwd, 0.0))
        left = sum(_dot(tri, p) for p in _split3(both))
        left_f, left_b = left[:, 0:1], left[:, 1:2]
        total_b = jnp.sum(lf_bwd, axis=0, keepdims=True)
    return left_f, total_b - left_b + lf_bwd


def _mlstm_state_step(k, v, ig_c, cum_c, total, c_ref, n_ref, m_ref):
    a = total - cum_c + ig_c
    m_loc = jnp.max(a, axis=0, keepdims=True)
    kw = k * jnp.exp(a - m_loc)
    kv = _dot_tn(kw.astype(BF16), v)
    ksum = jnp.sum(kw, axis=0, keepdims=True)
    m_old = m_ref[0:1, 0:1]
    m_new = jnp.maximum(total + m_old, m_loc)
    sp = jnp.exp(total + m_old - m_new)
    sc = jnp.exp(m_loc - m_new)
    c_ref[...] = sp * c_ref[...] + sc * kv
    n_ref[...] = sp * n_ref[...] + sc * jnp.broadcast_to(ksum, n_ref.shape)
    m_ref[...] = jnp.broadcast_to(m_new, m_ref.shape)


def _mlstm_output(qk, q, v, ig_r, cum_c, cum_r, c_prev, n_prev, m_prev, backward):
    t = q.shape[0]
    ri = lax.broadcasted_iota(jnp.int32, (t, t), 0)
    ci = lax.broadcasted_iota(jnp.int32, (t, t), 1)
    keep = (ri <= ci) if backward else (ri >= ci)
    dlog = jnp.where(keep, cum_c - cum_r + ig_r, -jnp.inf)
    inter = cum_c + m_prev
    m_t = jnp.maximum(inter, jnp.max(dlog, axis=-1, keepdims=True))
    s = qk * jnp.exp(dlog - m_t)
    wi = jnp.exp(inter - m_t)
    num = _dot(s.astype(BF16), v) + wi * _dot(q.astype(BF16), c_prev.astype(BF16))
    den = jnp.sum(s, axis=-1, keepdims=True) + wi * jnp.sum(q * n_prev, axis=-1, keepdims=True)
    return num / jnp.maximum(jnp.abs(den), jnp.exp(-m_t))


def _mlstm_kernel(bias_ref, q_ref, k_ref, v_ref, o_ref, gc_ref, gr_ref, gain_ref, out_ref,
                  cstore_ref, nstore_ref, mstore_ref, c_ref, n_ref, m_ref, *, nblk, cpb, t):
    h = pl.program_id(0)
    sweep = pl.program_id(1)
    i = pl.program_id(2)

    @pl.when(i == 0)
    def _():
        c_ref[...] = jnp.zeros_like(c_ref)
        n_ref[...] = jnp.zeros_like(n_ref)
        m_ref[...] = jnp.zeros_like(m_ref)

    @pl.when(sweep == 0)
    def _():
        def body(c, carry):
            cc = cpb - 1 - c
            rows = _chunk_rows(cc, t)
            n = (nblk - 1 - i) * cpb + cc
            cstore_ref[n] = c_ref[...]
            nstore_ref[n] = n_ref[...]
            mstore_ref[n] = m_ref[...]
            k = k_ref[rows, :].astype(F32) * (ML_DK ** -0.5)
            _, lf_c, _, _ = _mlstm_gates(gc_ref, gr_ref, bias_ref, h, 0, rows)
            ig_c, lb_c, _, _ = _mlstm_gates(gc_ref, gr_ref, bias_ref, h, 1, rows)
            _, cumb_c = _running_sums(lf_c, lb_c, False)
            total = jnp.sum(lb_c, axis=0, keepdims=True)
            _mlstm_state_step(k, v_ref[rows, :], ig_c, cumb_c, total, c_ref, n_ref, m_ref)
            return carry

        lax.fori_loop(0, cpb, body, 0)

    @pl.when(sweep == 1)
    def _():
        def body(c, carry):
            rows = _chunk_rows(c, t)
            n = i * cpb + c
            k = k_ref[rows, :].astype(F32) * (ML_DK ** -0.5)
            v = v_ref[rows, :]
            q = q_ref[rows, :].astype(F32)
            qk = _dot_nt(q_ref[rows, :], k.astype(BF16))
            ig_c, lf_c, ig_r, lf_r = _mlstm_gates(gc_ref, gr_ref, bias_ref, h, 0, rows)
            _, lb_c, igb_r, lb_r = _mlstm_gates(gc_ref, gr_ref, bias_ref, h, 1, rows)
            cum_c, cumb_c = _running_sums(lf_c, lb_c, False)
            cum_r, cumb_r = _running_sums(lf_r, lb_r, True)
            hf = _mlstm_output(qk, q, v, ig_r, cum_c, cum_r, c_ref[...], n_ref[0:1, :], m_ref[0:1, 0:1], False)
            total = jnp.sum(lf_c, axis=0, keepdims=True)
            _mlstm_state_step(k, v, ig_c, cum_c, total, c_ref, n_ref, m_ref)
            hb = _mlstm_output(qk, q, v, igb_r, cumb_c, cumb_r, cstore_ref[n], nstore_ref[n][0:1, :],
                               mstore_ref[n][0:1, 0:1], True)
            y = hf + hb
            y = y * lax.rsqrt(jnp.mean(y * y, axis=-1, keepdims=True) + EPS) * gain_ref[...]
            out_ref[rows, :] = (jax.nn.sigmoid(o_ref[rows, :].astype(F32)) * y).astype(out_ref.dtype)
            return carry

        lax.fori_loop(0, cpb, body, 0)


def mlstm_mixer(proj, gates, col0, gate_bias, norm_gain, t=CHUNK):
    seq = proj.shape[0]
    nc = seq // t
    cpb = math.gcd(CHUNKS_PER_STEP, nc)
    nblk = nc // cpb
    tb = cpb * t
    qb = col0 // ML_DK
    kb = qb + ML_H
    vb = (col0 + 2 * ML_H * ML_DK) // ML_DV
    ob = vb + ML_H
    g = gates[:, :4 * ML_H].reshape(seq, 2, 2, ML_H)
    g = jnp.transpose(g, (3, 0, 1, 2)).reshape(ML_H, seq, 4)
    g_cols = g
    g_rows = jnp.transpose(g, (0, 2, 1))

    def rows(sweep, i):
        return sweep * i + (1 - sweep) * (nblk - 1 - i)

    return pl.pallas_call(
        functools.partial(_mlstm_kernel, nblk=nblk, cpb=cpb, t=t),
        grid=(ML_H, 2, nblk),
        in_specs=[pl.BlockSpec(memory_space=pltpu.SMEM),
                  pl.BlockSpec((tb, ML_DK), lambda h, s, i: (s * i, qb + h)),
                  pl.BlockSpec((tb, ML_DK), lambda h, s, i: (rows(s, i), kb + h)),
                  pl.BlockSpec((tb, ML_DV), lambda h, s, i: (rows(s, i), vb + h)),
                  pl.BlockSpec((tb, ML_DV), lambda h, s, i: (s * i, ob + h)),
                  pl.BlockSpec((1, tb, 4), lambda h, s, i: (h, rows(s, i), 0)),
                  pl.BlockSpec((1, 4, tb), lambda h, s, i: (h, 0, rows(s, i))),
                  pl.BlockSpec((1, ML_DV), lambda h, s, i: (0, h))],
        out_specs=pl.BlockSpec((tb, ML_DV), lambda h, s, i: (s * i, h)),
        out_shape=jax.ShapeDtypeStruct((seq, ML_H * ML_DV), BF16),
        scratch_shapes=[pltpu.VMEM((nc, ML_DK, ML_DV), F32), pltpu.VMEM((nc, 8, ML_DK), F32),
                        pltpu.VMEM((nc, 8, LANES), F32), pltpu.VMEM((ML_DK, ML_DV), F32),
                        pltpu.VMEM((8, ML_DK), F32), pltpu.VMEM((8, LANES), F32)],
        compiler_params=_cparams("arbitrary", "arbitrary", "arbitrary"),
        name="mlstm",
    )(gate_bias, proj, proj, proj, proj, g_cols, g_rows, norm_gain.reshape(1, ML_H * ML_DV))


PERM_ROWS = 256
CLASS_RUN = BF16_TILE_ROWS
ATT_SUB_ROWS = 128


def _group_permutation(dilation):
    run = CLASS_RUN * dilation
    new = jnp.arange(PERM_ROWS, dtype=jnp.int32)
    within = new % run
    src = (new // run) * run + (within % CLASS_RUN) * dilation + within // CLASS_RUN
    return (src[:, None] == jnp.arange(PERM_ROWS, dtype=jnp.int32)[None, :]).astype(BF16)


def _attention_prep_kernel(gain_ref, p4_ref, p16_ref, x_ref, o1_ref, o4_ref, o16_ref):
    part = pl.program_id(1)

    @pl.when(part < 2)
    def _():
        gain = gain_ref[0]
        for h in range(ATT_H):
            cols = slice(h * ATT_DH, (h + 1) * ATT_DH)
            x = x_ref[:, cols].astype(F32)
            o1_ref[:, cols] = (x * lax.rsqrt(jnp.mean(x * x, axis=-1, keepdims=True) + EPS) * gain).astype(BF16)

    @pl.when(part == 2)
    def _():
        o1_ref[...] = x_ref[...]

    for s in range(x_ref.shape[0] // PERM_ROWS):
        rows = slice(s * PERM_ROWS, (s + 1) * PERM_ROWS)
        x = o1_ref[rows, :]
        o4_ref[rows, :] = _dot(p4_ref[...], x).astype(BF16)
        o16_ref[rows, :] = _dot(p16_ref[...], x).astype(BF16)


def attention_prep(proj, col0, qk_gain, tr=512):
    seq = proj.shape[0]
    cb = col0 // W_GROUP
    gains = jnp.stack([qk_gain[0] * (ATT_DH ** -0.5), qk_gain[1], jnp.ones_like(qk_gain[0])]).reshape(3, 1, ATT_DH)
    out = jax.ShapeDtypeStruct((seq, 3 * W_GROUP), BF16)
    ospec = pl.BlockSpec((tr, W_GROUP), lambda i, j: (i, j))
    pspec = pl.BlockSpec((PERM_ROWS, PERM_ROWS), lambda i, j: (0, 0))
    return pl.pallas_call(
        _attention_prep_kernel,
        grid=(seq // tr, 3),
        in_specs=[pl.BlockSpec((1, 1, ATT_DH), lambda i, j: (j, 0, 0)), pspec, pspec,
                  pl.BlockSpec((tr, W_GROUP), lambda i, j: (i, cb + j))],
        out_specs=[ospec, ospec, ospec],
        out_shape=[out, out, out],
        compiler_params=_cparams("parallel", "arbitrary"),
        name="attention_prep",
    )(gains.astype(F32), _group_permutation(4), _group_permutation(16), proj)


def _band_attention_kernel(q_ref, kp_ref, kc_ref, kn_ref, vp_ref, vc_ref, vn_ref, o_ref, lse_ref,
                           *, dilation, nblk):
    i = pl.program_id(1)
    tq = q_ref.shape[0] * CLASS_RUN
    hs = ATT_HALF_STEPS
    sq = min(ATT_SUB_ROWS, tq)
    sk = sq + 2 * hs
    ri = lax.broadcasted_iota(jnp.int32, (sq, sk), 0)
    ci = lax.broadcasted_iota(jnp.int32, (sq, sk), 1)
    off = ci - hs - ri
    first_col = jnp.where(i > 0, 0, hs)
    end_col = jnp.where(i < nblk - 1, tq + 2 * hs, tq + hs)
    in_band = jnp.abs(off) <= hs
    dist = (jnp.abs(off) * dilation).astype(F32)
    valid = [in_band & (ci + s0 >= first_col) & (ci + s0 < end_col) for s0 in range(0, tq, sq)]
    lane = lax.broadcasted_iota(jnp.int32, (sq, LANES), 1)
    lse_all = [jnp.zeros((sq, LANES), F32) for _ in valid]

    def rows(ref, cols):
        x = ref[:, :, cols]
        return x.reshape(x.shape[0] * CLASS_RUN, x.shape[2])

    for h in range(ATT_H):
        cols = slice(h * ATT_DH, (h + 1) * ATT_DH)
        slope = 2.0 ** (-8.0 * (h + 1) / ATT_H)
        qq = rows(q_ref, cols)
        kk = jnp.concatenate([rows(kp_ref, cols), rows(kc_ref, cols), rows(kn_ref, cols)], axis=0)
        vv = jnp.concatenate([rows(vp_ref, cols), rows(vc_ref, cols), rows(vn_ref, cols)], axis=0)
        outs = []
        for b, s0 in enumerate(range(0, tq, sq)):
            s = _dot_nt(qq[s0:s0 + sq], kk[s0:s0 + sk]) - slope * dist
            s = jnp.where(valid[b], s, NEG)
            m = jnp.max(s, axis=-1, keepdims=True)
            p = jnp.exp(s - m)
            den = jnp.sum(p, axis=-1, keepdims=True)
            outs.append(_dot(p.astype(BF16), vv[s0:s0 + sk]) / den)
            lse_all[b] = jnp.where(lane == h, m + jnp.log(den), lse_all[b])
        o = jnp.concatenate(outs, axis=0)
        o_ref[:, :, cols] = o.astype(o_ref.dtype).reshape(tq // CLASS_RUN, CLASS_RUN, ATT_DH)
    lse_ref[...] = jnp.concatenate(lse_all, axis=0).reshape(tq // CLASS_RUN, CLASS_RUN, LANES)


def band_attention(qkv, dilation, tq=256):
    seq = qkv.shape[0]
    n = seq // dilation
    tq = min(tq, n)
    nblk = n // tq
    hs = ATT_HALF_STEPS
    runs = n // CLASS_RUN
    tr = tq // CLASS_RUN
    hr = hs // CLASS_RUN
    ratio = tq // hs
    last_halo = n // hs - 1
    view = qkv.reshape(runs, dilation, CLASS_RUN, 3 * W_GROUP)

    def cur(part):
        return pl.BlockSpec((tr, None, CLASS_RUN, W_GROUP), lambda r, i: (i, r, 0, part))

    def prev(part):
        return pl.BlockSpec((hr, None, CLASS_RUN, W_GROUP), lambda r, i: (jnp.maximum(i * ratio - 1, 0), r, 0, part))

    def nxt(part):
        return pl.BlockSpec((hr, None, CLASS_RUN, W_GROUP),
                            lambda r, i: (jnp.minimum((i + 1) * ratio, last_halo), r, 0, part))

    o, lse = pl.pallas_call(
        functools.partial(_band_attention_kernel, dilation=dilation, nblk=nblk),
        grid=(dilation, nblk),
        in_specs=[cur(0), prev(1), cur(1), nxt(1), prev(2), cur(2), nxt(2)],
        out_specs=[pl.BlockSpec((tr, None, CLASS_RUN, W_GROUP), lambda r, i: (i, r, 0, 0)),
                   pl.BlockSpec((tr, None, CLASS_RUN, LANES), lambda r, i: (i, r, 0, 0))],
        out_shape=[jax.ShapeDtypeStruct((runs, dilation, CLASS_RUN, W_GROUP), BF16),
                   jax.ShapeDtypeStruct((runs, dilation, CLASS_RUN, LANES), F32)],
        compiler_params=_cparams("parallel", "arbitrary"),
        name=f"band_attention_d{dilation}",
    )(view, view, view, view, view, view, view)
    return o.reshape(seq, W_GROUP), lse.reshape(seq, LANES)


def _merge_branches_kernel(q4_ref, q16_ref, o1_ref, o4_ref, o16_ref, l1_ref, l4_ref, l16_ref, out_ref):
    def ungroup(qt, x):
        return _dot(qt, x)

    def ungroup_f32(qt, x):
        hi = x.astype(BF16)
        lo = (x - hi.astype(F32)).astype(BF16)
        return _dot(qt, hi) + _dot(qt, lo)

    for s in range(o1_ref.shape[0] // PERM_ROWS):
        rows = slice(s * PERM_ROWS, (s + 1) * PERM_ROWS)
        q4, q16 = q4_ref[...], q16_ref[...]
        l1 = l1_ref[rows, :]
        l2 = ungroup_f32(q4, l4_ref[rows, :])
        l3 = ungroup_f32(q16, l16_ref[rows, :])
        m = jnp.maximum(jnp.maximum(l1, l2), l3)
        e1, e2, e3 = jnp.exp(l1 - m), jnp.exp(l2 - m), jnp.exp(l3 - m)
        inv = 1.0 / (e1 + e2 + e3)
        w1, w2, w3 = e1 * inv, e2 * inv, e3 * inv
        o2 = ungroup(q4, o4_ref[rows, :])
        o3 = ungroup(q16, o16_ref[rows, :])
        for h in range(ATT_H):
            cols = slice(h * ATT_DH, (h + 1) * ATT_DH)
            out_ref[rows, cols] = (w1[:, h:h + 1] * o1_ref[rows, cols].astype(F32)
                                   + w2[:, h:h + 1] * o2[:, cols]
                                   + w3[:, h:h + 1] * o3[:, cols]).astype(out_ref.dtype)


def merge_branches(outs, lses, tr=512):
    seq = outs[0].shape[0]
    ospec = pl.BlockSpec((tr, W_GROUP), lambda i: (i, 0))
    lspec = pl.BlockSpec((tr, LANES), lambda i: (i, 0))
    pspec = pl.BlockSpec((PERM_ROWS, PERM_ROWS), lambda i: (0, 0))
    return pl.pallas_call(
        _merge_branches_kernel,
        grid=(seq // tr,),
        in_specs=[pspec, pspec, ospec, ospec, ospec, lspec, lspec, lspec],
        out_specs=ospec,
        out_shape=jax.ShapeDtypeStruct((seq, W_GROUP), BF16),
        compiler_params=_cparams("parallel"),
        name="attention_merge",
    )(_group_permutation(4).T, _group_permutation(16).T, *outs, *lses)


def dilated_attention(proj, col0, qk_gain):
    grouped = attention_prep(proj, col0, qk_gain)
    outs, lses = [], []
    for d, qkv in zip(ATT_DILATIONS, grouped):
        o, lse = band_attention(qkv, d)
        outs.append(o)
        lses.append(lse)
    return merge_branches(outs, lses)


def kernel(x, c, ada_w, ada_b, ada_table, w_in, w_out, hy_short, hy_w1, hy_b1, hy_w2, hy_b2, hy_w3, hy_bias,
           ret_decay, att_qk_gain, ml_gate_bias, ml_norm_gain, ffn_w1, ffn_w3, ffn_w2):
    batch, seq, d_model = x.shape
    depth = w_in.shape[0]
    d_main = 12 * W_GROUP
    hidden = ffn_w1.shape[2]

    tabs = _dft_tables(seq)
    consts = _filter_positions(seq)
    mod_shared = ada_modulation(c, ada_w, ada_b)

    w_in_b = w_in.astype(BF16)
    w_gate_b = jnp.pad(w_in[:, :, d_main:], ((0, 0), (0, 0), (0, LANES - (w_in.shape[2] - d_main)))).astype(BF16)
    w_out_b = w_out.astype(BF16)
    ffn_first = (ffn_w1[0].astype(BF16), ffn_w3[0].astype(BF16), ffn_w2[0].astype(BF16))

    rows = x.reshape(batch * seq, d_model)
    outs = []
    for b in range(batch):
        xb = rows[b * seq:(b + 1) * seq]
        w1_b, w3_b, w2_b = ffn_first
        for l in range(depth):
            mod = (mod_shared[b:b + 1] + ada_table[l].reshape(1, -1)).reshape(6, d_model)
            sh1, sc1, g1, sh2, sc2, g2 = (mod[i:i + 1] for i in range(6))
            proj, gates = norm_proj(xb, 1.0 + sc1, sh1, w_in_b, l, d_main, w_gate_b[l])

            kf = hyena_filter_spectra(seq, tabs, consts, hy_w1[l], hy_b1[l], hy_w2[l], hy_b2[l], hy_w3[l])
            y_a = hyena_mixer(proj, tabs, kf, hy_short[l], hy_bias[l])
            y_b = retention_mixer(proj, 3 * W_GROUP, jax.nn.log_sigmoid(ret_decay[l].astype(F32)))
            y_c = dilated_attention(proj, 6 * W_GROUP, att_qk_gain[l])
            y_d = mlstm_mixer(proj, gates, 9 * W_GROUP, ml_gate_bias[l], ml_norm_gain[l])
            y = jnp.concatenate([y_a.astype(BF16), y_b, y_c, y_d], axis=-1)
            xb = mm_residual(y, w_out_b, l, xb, g1, tm=1024, tn=1024, tk=y.shape[1])

            nxt = l + 1
            to_cast = ((ffn_w1, nxt, False), (ffn_w3, nxt, False), (ffn_w2, nxt, True)) if nxt < depth else ()
            u, converted = norm_swiglu(xb, 1.0 + sc2, sh2, w1_b, w3_b, to_cast)
            xb = mm_residual(u, w2_b[None], 0, xb, g2, tm=1024, tn=512, tk=hidden // 2)
            if converted:
                w1_b, w3_b, w2_b = converted
        outs.append(xb)
    return jnp.concatenate(outs, 0).reshape(batch, seq, d_model)
```

```python
import functools
import math

import numpy as np
import jax
import jax.numpy as jnp
from jax import lax
from jax.experimental import pallas as pl
from jax.experimental.pallas import tpu as pltpu

F32 = jnp.float32
BF16 = jnp.bfloat16
HIGHEST = lax.Precision.HIGHEST

EPS = 1e-6
NEG = -1e30

V7X_VMEM_LIMIT_BYTES = 56 * 1024 * 1024
LANES = 128
BF16_TILE_ROWS = 16

W_GROUP = 1024
RET_H, RET_DK, RET_DV = 4, 128, 256
ATT_H, ATT_DH = 8, 128
ATT_HALF_STEPS = 64
ATT_DILATIONS = (1, 4, 16)
ML_H, ML_DK, ML_DV = 4, 128, 256
HY_BANDS = 16
HY_W = W_GROUP
HY_FILTER_WIDTH = 64
HY_TARGET, HY_SHORT_PCT, HY_LONG_PCT = 1e-2, 0.3, 1.5
HY_FEAT_PAD = 40
DFT_N2 = 128
CHUNK = 256


def _cparams(*sem):
    return pltpu.CompilerParams(dimension_semantics=sem, vmem_limit_bytes=V7X_VMEM_LIMIT_BYTES)


def _dot(a, b):
    return jnp.dot(a, b, preferred_element_type=F32)


def _dot_nt(a, b):
    return lax.dot_general(a, b, (((1,), (1,)), ((), ())), preferred_element_type=F32)


def _dot_tn(a, b):
    return lax.dot_general(a, b, (((0,), (0,)), ((), ())), preferred_element_type=F32)


def _ada_kernel(c_ref, w_ref, b_ref, o_ref):
    c = c_ref[...]
    s = c * jax.nn.sigmoid(c)
    o_ref[...] = jnp.dot(s, w_ref[...], preferred_element_type=F32, precision=HIGHEST) + b_ref[...]


def ada_modulation(c, ada_w, ada_b):
    d, n = ada_w.shape
    tn = 512
    c8 = jnp.broadcast_to(c.reshape(1, d), (8, d))
    out = pl.pallas_call(
        _ada_kernel,
        grid=(n // tn,),
        in_specs=[pl.BlockSpec((8, d), lambda j: (0, 0)),
                  pl.BlockSpec((d, tn), lambda j: (0, j)),
                  pl.BlockSpec((1, tn), lambda j: (0, j))],
        out_specs=pl.BlockSpec((8, tn), lambda j: (0, j)),
        out_shape=jax.ShapeDtypeStruct((8, n), F32),
        compiler_params=_cparams("parallel"),
        name="ada_modulation",
    )(c8, ada_w, ada_b.reshape(1, n))
    return out[0:1]


NORM_ROWS = 16


def _normalise_into(x_ref, sc_ref, sh_ref, h_ref):
    tm = x_ref.shape[0]
    sc = sc_ref[...]
    sh = sh_ref[...]

    def body(r, carry):
        rows = pl.ds(pl.multiple_of(r * NORM_ROWS, NORM_ROWS), NORM_ROWS)
        x = x_ref[rows, :]
        ms = jnp.mean(x * x, axis=-1, keepdims=True)
        h_ref[rows, :] = (x * lax.rsqrt(ms + EPS) * sc + sh).astype(h_ref.dtype)
        return carry

    lax.fori_loop(0, tm // NORM_ROWS, body, 0, unroll=4)


def _norm_proj_kernel(x_ref, sc_ref, sh_ref, w_ref, wg_ref, o_ref, g_ref, h_ref):
    @pl.when(pl.program_id(1) == 0)
    def _():
        _normalise_into(x_ref, sc_ref, sh_ref, h_ref)
        g_ref[...] = _dot(h_ref[...], wg_ref[...])

    o_ref[...] = _dot(h_ref[...], w_ref[...]).astype(o_ref.dtype)


def norm_proj(x, scale1p, shift, w_stack, layer, n, wg, tm=1024, tn=1024):
    m, d = x.shape
    ng = wg.shape[1]
    return pl.pallas_call(
        _norm_proj_kernel,
        grid=(m // tm, n // tn),
        in_specs=[pl.BlockSpec((tm, d), lambda i, j: (i, 0), pipeline_mode=pl.Buffered(1)),
                  pl.BlockSpec((1, d), lambda i, j: (0, 0)),
                  pl.BlockSpec((1, d), lambda i, j: (0, 0)),
                  pl.BlockSpec((None, d, tn), lambda i, j: (layer, 0, j)),
                  pl.BlockSpec((d, ng), lambda i, j: (0, 0))],
        out_specs=[pl.BlockSpec((tm, tn), lambda i, j: (i, j)),
                   pl.BlockSpec((tm, ng), lambda i, j: (i, 0))],
        out_shape=[jax.ShapeDtypeStruct((m, n), BF16), jax.ShapeDtypeStruct((m, ng), F32)],
        scratch_shapes=[pltpu.VMEM((tm, d), BF16)],
        compiler_params=_cparams("parallel", "arbitrary"),
        name="norm_proj",
    )(x, scale1p, shift, w_stack, wg)


def _norm_swiglu_kernel(x_ref, sc_ref, sh_ref, w1_ref, w3_ref, *rest, n_cast):
    cast_src, o_ref, cast_dst, h_ref = rest[:n_cast], rest[n_cast], rest[n_cast + 1:2 * n_cast + 1], rest[-1]

    @pl.when(pl.program_id(1) == 0)
    def _():
        _normalise_into(x_ref, sc_ref, sh_ref, h_ref)

    h = h_ref[...]
    a = _dot(h, w1_ref[...])
    b = _dot(h, w3_ref[...])
    o_ref[...] = (a * jax.nn.sigmoid(a) * b).astype(o_ref.dtype)
    for src, dst in zip(cast_src, cast_dst):
        dst[...] = src[...].astype(dst.dtype)


def norm_swiglu(x, scale1p, shift, w1, w3, to_cast=(), tm=1024, tn=256):
    m, d = x.shape
    n = w1.shape[1]
    gi, gj = m // tm, n // tn
    cast_in_specs, cast_out_specs, cast_shapes = [], [], []
    for stack, layer, swap in to_cast:
        rows, cols = stack.shape[1:]
        br, bc = (rows // gj, cols // gi) if swap else (rows // gi, cols // gj)
        idx = (lambda i, j: (j, i)) if swap else (lambda i, j: (i, j))
        cast_in_specs.append(pl.BlockSpec((None, br, bc), lambda i, j, idx=idx, layer=layer: (layer,) + idx(i, j)))
        cast_out_specs.append(pl.BlockSpec((br, bc), idx))
        cast_shapes.append(jax.ShapeDtypeStruct((rows, cols), BF16))
    res = pl.pallas_call(
        functools.partial(_norm_swiglu_kernel, n_cast=len(to_cast)),
        grid=(gi, gj),
        in_specs=[pl.BlockSpec((tm, d), lambda i, j: (i, 0)),
                  pl.BlockSpec((1, d), lambda i, j: (0, 0)),
                  pl.BlockSpec((1, d), lambda i, j: (0, 0)),
                  pl.BlockSpec((d, tn), lambda i, j: (0, j)),
                  pl.BlockSpec((d, tn), lambda i, j: (0, j))] + cast_in_specs,
        out_specs=[pl.BlockSpec((tm, tn), lambda i, j: (i, j))] + cast_out_specs,
        out_shape=[jax.ShapeDtypeStruct((m, n), BF16)] + cast_shapes,
        scratch_shapes=[pltpu.VMEM((tm, d), BF16)],
        compiler_params=_cparams("parallel", "arbitrary"),
        name="norm_swiglu",
    )(x, scale1p, shift, w1, w3, *[stack for stack, _, _ in to_cast])
    return res[0], res[1:]


def _mm_residual_kernel(y_ref, w_ref, x_ref, g_ref, *rest, n_cast):
    cast_src, o_ref, cast_dst = rest[:n_cast], rest[n_cast], rest[n_cast + 1:]
    part = g_ref[...] * _dot(y_ref[...], w_ref[...])

    @pl.when(pl.program_id(2) == 0)
    def _():
        o_ref[...] = x_ref[...] + part

    @pl.when(pl.program_id(2) > 0)
    def _():
        o_ref[...] += part

    for src, dst in zip(cast_src, cast_dst):
        dst[...] = src[...].astype(dst.dtype)


def _mm_residual_1k_kernel(y_ref, w_ref, x_ref, g_ref, o_ref):
    o_ref[...] = x_ref[...] + g_ref[...] * _dot(y_ref[...], w_ref[...])


def mm_residual(y, w_stack, layer, x, gate, tm, tn, tk, to_cast=()):
    m, kk = y.shape
    n = w_stack.shape[2]
    nk = kk // tk
    if nk == 1:
        assert not to_cast
        return pl.pallas_call(
            _mm_residual_1k_kernel,
            grid=(m // tm, n // tn),
            in_specs=[pl.BlockSpec((tm, kk), lambda i, j: (i, 0)),
                      pl.BlockSpec((None, kk, tn), lambda i, j: (layer, 0, j)),
                      pl.BlockSpec((tm, tn), lambda i, j: (i, j)),
                      pl.BlockSpec((1, tn), lambda i, j: (0, j))],
            out_specs=pl.BlockSpec((tm, tn), lambda i, j: (i, j)),
            out_shape=jax.ShapeDtypeStruct((m, n), F32),
            compiler_params=_cparams("parallel", "arbitrary"),
            name="mm_residual",
        )(y, w_stack, x, gate), ()
    gi, gjk = m // tm, (n // tn) * nk
    cast_in_specs, cast_out_specs, cast_shapes = [], [], []
    for stack, cast_layer, cols in to_cast:
        br, bc = stack.shape[1] // gi, cols // gjk
        cast_in_specs.append(pl.BlockSpec((None, br, bc),
                                          lambda i, j, k, cast_layer=cast_layer: (cast_layer, i, j * nk + k)))
        cast_out_specs.append(pl.BlockSpec((br, bc), lambda i, j, k: (i, j * nk + k)))
        cast_shapes.append(jax.ShapeDtypeStruct((stack.shape[1], cols), BF16))
    res = pl.pallas_call(
        functools.partial(_mm_residual_kernel, n_cast=len(to_cast)),
        grid=(gi, n // tn, nk),
        in_specs=[pl.BlockSpec((tm, tk), lambda i, j, k: (i, k)),
                  pl.BlockSpec((None, tk, tn), lambda i, j, k: (layer, k, j)),
                  pl.BlockSpec((tm, tn), lambda i, j, k: (i, j)),
                  pl.BlockSpec((1, tn), lambda i, j, k: (0, j))] + cast_in_specs,
        out_specs=[pl.BlockSpec((tm, tn), lambda i, j, k: (i, j))] + cast_out_specs,
        out_shape=[jax.ShapeDtypeStruct((m, n), F32)] + cast_shapes,
        compiler_params=_cparams("parallel", "arbitrary", "arbitrary"),
        name="mm_residual_ksplit",
    )(y, w_stack, x, gate, *[stack for stack, _, _ in to_cast])
    return res[0], res[1:]


def _dft_tables(seq_len):
    n = 2 * seq_len
    n2 = DFT_N2
    n1 = n // n2
    half = n1 // 2
    kb = half + 1
    kbp = -(-kb // 8) * 8
    k1 = jnp.arange(kbp, dtype=jnp.int32)
    live = (k1 <= half)
    col = jnp.arange(n1, dtype=jnp.int32)
    ph = (2.0 * math.pi / n1) * ((k1[:, None] * col[None, :]) % n1).astype(F32)
    wf = jnp.stack([jnp.cos(ph), -jnp.sin(ph)], axis=0) * live[None, :, None]
    wf = wf.reshape(2 * kbp, n1)
    a = jnp.arange(n2, dtype=jnp.int32)
    mm = (a[None, None, :] * (k1[:, None, None] + n1 * a[None, :, None])) % n
    th = (2.0 * math.pi / n) * mm.astype(F32)
    c, s = jnp.cos(th), jnp.sin(th)
    tf = jnp.concatenate([jnp.concatenate([c, s], 2), jnp.concatenate([-s, c], 2)], 1)
    tf = tf * live[:, None, None]
    ti = jnp.transpose(tf, (0, 2, 1))
    wt = jnp.where((k1 == 0) | (k1 == half), 1.0, 2.0) * live / n
    row = jnp.arange(half, dtype=jnp.int32)
    ph2 = (2.0 * math.pi / n1) * ((row[:, None] * k1[None, :]) % n1).astype(F32)
    return dict(wf_data=wf[:, :half].astype(BF16), wf_filt=wf.astype(BF16), tf=tf.astype(BF16),
                ti=ti.astype(BF16), wi_re=(wt * jnp.cos(ph2)).astype(BF16), wi_im=(-wt * jnp.sin(ph2)).astype(BF16),
                kbp=kbp, n1=n1)


INNER_STEP = 8


U32 = jnp.uint32


def _pack_pair(hi, lo):
    hb = lax.bitcast_convert_type(hi.astype(BF16).astype(F32), U32)
    lb = lax.bitcast_convert_type(lo.astype(BF16).astype(F32), U32)
    return hb | (lb >> 16)


def _unpack_pair(w):
    hi = lax.bitcast_convert_type(w & jnp.uint32(0xFFFF0000), F32).astype(BF16)
    lo = lax.bitcast_convert_type(w << 16, F32).astype(BF16)
    return hi, lo


def _outer_dft_kernel(w_ref, x_ref, o_ref, *, packed_in):
    kbp = o_ref.shape[0]
    w = w_ref[...]
    for j in range(x_ref.shape[0]):
        if packed_in:
            parts = _unpack_pair(x_ref[j])
        else:
            parts = (x_ref[j].astype(BF16),)
        tc = parts[0].shape[1]
        for p, x in enumerate(parts):
            a = _dot(w, x)
            o_ref[:, j, p * tc:(p + 1) * tc] = _pack_pair(a[:kbp], a[kbp:])


def outer_dft(w, x, tc=1024):
    r2, k = w.shape
    inner, _, ch = x.shape
    packed_in = x.dtype == U32
    mult = 2 if packed_in else 1
    return pl.pallas_call(
        functools.partial(_outer_dft_kernel, packed_in=packed_in),
        grid=(inner // INNER_STEP, ch // tc),
        in_specs=[pl.BlockSpec((r2, k), lambda i, j: (0, 0)),
                  pl.BlockSpec((INNER_STEP, k, tc), lambda i, j: (i, 0, j))],
        out_specs=pl.BlockSpec((r2 // 2, INNER_STEP, mult * tc), lambda i, j: (0, i, j)),
        out_shape=jax.ShapeDtypeStruct((r2 // 2, inner, mult * ch), U32),
        compiler_params=_cparams("parallel", "parallel"),
        name="hyena_outer_dft",
    )(w, x)


def _filter_spectrum_kernel(tf_ref, a_ref, inv_ref, o_ref):
    inv = inv_ref[...]
    for b in range(tf_ref.shape[0]):
        x = jnp.concatenate(_unpack_pair(a_ref[b]), axis=0)
        o_ref[b] = (_dot(tf_ref[b], x) * inv).astype(o_ref.dtype)


def filter_spectrum(tf, a, inv_l1, kb=4, tc=1024):
    kbp, inner, ch = a.shape
    r = 2 * inner
    return pl.pallas_call(
        _filter_spectrum_kernel,
        grid=(kbp // kb, ch // tc),
        in_specs=[pl.BlockSpec((kb, r, r), lambda i, j: (i, 0, 0)),
                  pl.BlockSpec((kb, inner, tc), lambda i, j: (i, 0, j)),
                  pl.BlockSpec((1, tc), lambda i, j: (0, j))],
        out_specs=pl.BlockSpec((kb, r, tc), lambda i, j: (i, 0, j)),
        out_shape=jax.ShapeDtypeStruct((kbp, r, ch), BF16),
        compiler_params=_cparams("parallel", "arbitrary"),
        name="hyena_filter_spectrum",
    )(tf, a, inv_l1)


def _spectral_conv_kernel(tf_ref, ti_ref, a_ref, k_ref, g_ref):
    half = a_ref.shape[1]
    for b in range(tf_ref.shape[0]):
        x = jnp.concatenate(_unpack_pair(a_ref[b]), axis=0)
        z = _dot(tf_ref[b], x)
        zr, zi = z[:half], z[half:]
        kr = k_ref[b, :half, :].astype(F32)
        ki = k_ref[b, half:, :].astype(F32)
        y = jnp.concatenate([zr * kr - zi * ki, zr * ki + zi * kr], axis=0).astype(BF16)
        g = _dot(ti_ref[b], y)
        g_ref[:, b, :] = _pack_pair(g[:half], g[half:])


def spectral_conv(tf, ti, a, kf, order, tc=512):
    kbp, inner, ch = a.shape
    r = 2 * inner
    kb = INNER_STEP
    ncb = ch // tc
    return pl.pallas_call(
        _spectral_conv_kernel,
        grid=(kbp // kb, ncb),
        in_specs=[pl.BlockSpec((kb, r, r), lambda i, j: (i, 0, 0)),
                  pl.BlockSpec((kb, r, r), lambda i, j: (i, 0, 0)),
                  pl.BlockSpec((kb, inner, tc), lambda i, j: (i, 0, j)),
                  pl.BlockSpec((kb, r, tc), lambda i, j: (i, 0, order * ncb + j))],
        out_specs=pl.BlockSpec((inner, kb, tc), lambda i, j: (0, i, j)),
        out_shape=jax.ShapeDtypeStruct((inner, kbp, ch), U32),
        compiler_params=_cparams("parallel", "arbitrary"),
        name="hyena_spectral_conv",
    )(tf, ti, a, kf)


def _outer_inverse_gate_kernel(wre_ref, wim_ref, g_ref, gate_ref, zp_ref, bias_ref, o_ref, *, token_major_out):
    wre = wre_ref[...]
    wim = wim_ref[...]
    bias = bias_ref[...]
    for j in range(g_ref.shape[0]):
        g_re, g_im = _unpack_pair(g_ref[j])
        z = gate_ref[j] * (_dot(wre, g_re) + _dot(wim, g_im) + bias * zp_ref[j])
        if token_major_out:
            o_ref[:, j, :] = z
        else:
            o_ref[j] = z


def outer_inverse_gate(wi_re, wi_im, g, gate, z_prev, bias, token_major_out, tc=512):
    r, kbp = wi_re.shape
    inner, _, ch = g.shape
    wspec = pl.BlockSpec((r, kbp), lambda i, j: (0, 0))
    zspec = pl.BlockSpec((INNER_STEP, r, tc), lambda i, j: (i, 0, j))
    if token_major_out:
        ospec, oshape = pl.BlockSpec((r, INNER_STEP, tc), lambda i, j: (0, i, j)), (r, inner, ch)
    else:
        ospec, oshape = zspec, (inner, r, ch)
    return pl.pallas_call(
        functools.partial(_outer_inverse_gate_kernel, token_major_out=token_major_out),
        grid=(inner // INNER_STEP, ch // tc),
        in_specs=[wspec, wspec, pl.BlockSpec((INNER_STEP, kbp, tc), lambda i, j: (i, 0, j)), zspec, zspec,
                  pl.BlockSpec((1, tc), lambda i, j: (0, j))],
        out_specs=ospec,
        out_shape=jax.ShapeDtypeStruct(oshape, F32),
        compiler_params=_cparams("parallel", "parallel"),
        name="hyena_outer_inverse_gate",
    )(wi_re, wi_im, g, gate, z_prev, bias)


def _short_conv_kernel(u_ref, prev_ref, next_ref, w_ref, v_ref, x1_ref, x2_ref, *, nb):
    i = pl.program_id(0)
    tr = u_ref.shape[0]
    ch = v_ref.shape[2]
    halo = prev_ref.shape[0]
    row = lax.broadcasted_iota(jnp.int32, (tr, ch), 0)
    has_prev = jnp.where(i > 0, 1.0, 0.0)
    has_next = jnp.where(i < nb - 1, 1.0, 0.0)
    for part, o_ref in enumerate((v_ref, x1_ref, x2_ref)):
        cols = slice(part * ch, (part + 1) * ch)
        x = u_ref[:, cols].astype(F32)
        prev_row = prev_ref[halo - 1:halo, cols].astype(F32) * has_prev
        next_row = next_ref[0:1, cols].astype(F32) * has_next
        before = jnp.where(row == 0, prev_row, pltpu.roll(x, 1, axis=0))
        after = jnp.where(row == tr - 1, next_row, pltpu.roll(x, tr - 1, axis=0))
        w = w_ref[:, cols]
        y = before * w[0:1] + x * w[1:2] + after * w[2:3]
        for j in range(tr // DFT_N2):
            o_ref[:, j, :] = y[j * DFT_N2:(j + 1) * DFT_N2]


def short_conv(proj, short_w):
    seq = proj.shape[0]
    width = 3 * HY_W
    tr = INNER_STEP * DFT_N2
    nb = seq // tr
    halo = BF16_TILE_ROWS
    per = tr // halo
    out = jax.ShapeDtypeStruct((DFT_N2, seq // DFT_N2, HY_W), F32)
    return pl.pallas_call(
        functools.partial(_short_conv_kernel, nb=nb),
        grid=(nb,),
        in_specs=[pl.BlockSpec((tr, width), lambda i: (i, 0)),
                  pl.BlockSpec((halo, width), lambda i: (jnp.maximum(i * per - 1, 0), 0)),
                  pl.BlockSpec((halo, width), lambda i: (jnp.minimum((i + 1) * per, nb * per - 1), 0)),
                  pl.BlockSpec((3, width), lambda i: (0, 0))],
        out_specs=[pl.BlockSpec((DFT_N2, INNER_STEP, HY_W), lambda i: (0, i, 0))] * 3,
        out_shape=[out, out, out],
        compiler_params=_cparams("parallel"),
        name="hyena_short_conv",
    )(proj, proj, proj, short_w)


def _filter_mlp_kernel(z_ref, w1_ref, b1_ref, w2_ref, b2_ref, o_ref):
    hdot = functools.partial(jnp.dot, preferred_element_type=F32, precision=HIGHEST)
    h = jnp.sin(hdot(w1_ref[...], z_ref[...]) + b1_ref[...])
    for i in range(w2_ref.shape[0]):
        h = jnp.sin(hdot(w2_ref[i], h) + b2_ref[i])
    o_ref[...] = h


def filter_mlp(feat_t, w1t, b1, w2t, b2, tc=2048):
    fp, npos = feat_t.shape
    wd = w1t.shape[0]
    ni = w2t.shape[0]
    tc = min(tc, npos)
    return pl.pallas_call(
        _filter_mlp_kernel,
        grid=(npos // tc,),
        in_specs=[pl.BlockSpec((fp, tc), lambda i: (0, i)),
                  pl.BlockSpec((wd, fp), lambda i: (0, 0)),
                  pl.BlockSpec((wd, 1), lambda i: (0, 0)),
                  pl.BlockSpec((ni, wd, wd), lambda i: (0, 0, 0)),
                  pl.BlockSpec((ni, wd, 1), lambda i: (0, 0, 0))],
        out_specs=pl.BlockSpec((wd, tc), lambda i: (0, i)),
        out_shape=jax.ShapeDtypeStruct((wd, npos), F32),
        compiler_params=_cparams("parallel"),
        name="hyena_filter_mlp",
    )(feat_t, w1t, b1.reshape(wd, 1), w2t, b2.reshape(ni, wd, 1))


def _filter_expand_kernel(h_ref, t_ref, w3_ref, dl_ref, o_ref, s_ref):
    @pl.when(pl.program_id(0) == 0)
    def _():
        s_ref[...] = jnp.zeros_like(s_ref)

    f = _dot_tn(h_ref[...].astype(BF16), w3_ref[0].astype(BF16))
    f = f * (jnp.exp(-t_ref[:, 0:1] * dl_ref[...]) * t_ref[:, 1:2])
    tr, ch = f.shape
    words = _pack_pair(f[:, :ch // 2], f[:, ch // 2:])
    for j in range(tr // DFT_N2):
        o_ref[:, j, :] = words[j * DFT_N2:(j + 1) * DFT_N2]
    s_ref[...] += jnp.sum(jnp.abs(f).reshape(tr // 8, 8, ch), axis=0)


def filter_expand(h_t, t_mask, w3_halves, deltas2):
    wd, n = h_t.shape
    ch = w3_halves.shape[2]
    tr = INNER_STEP * DFT_N2
    nb = n // tr
    return pl.pallas_call(
        _filter_expand_kernel,
        grid=(nb,),
        in_specs=[pl.BlockSpec((wd, tr), lambda i: (0, i)),
                  pl.BlockSpec((tr, 2), lambda i: (i, 0)),
                  pl.BlockSpec((1, wd, ch), lambda i: (i // (nb // 2), 0, 0)),
                  pl.BlockSpec((1, ch), lambda i: (0, 0))],
        out_specs=[pl.BlockSpec((DFT_N2, INNER_STEP, ch // 2), lambda i: (0, i, 0)),
                   pl.BlockSpec((8, ch), lambda i: (0, 0))],
        out_shape=[jax.ShapeDtypeStruct((DFT_N2, n // DFT_N2, ch // 2), U32), jax.ShapeDtypeStruct((8, ch), F32)],
        compiler_params=_cparams("arbitrary"),
        name="hyena_filter_expand",
    )(h_t, t_mask, w3_halves, deltas2)


def _filter_positions(seq_len):
    pos = np.arange(seq_len, dtype=np.float64)
    t = np.linspace(0.0, 1.0, seq_len)
    ang = (2.0 * math.pi / seq_len) * pos
    freqs = np.linspace(1e-4, HY_BANDS - 1, HY_BANDS)
    feat = np.concatenate([t[:, None], np.cos(ang[:, None] * freqs), -np.sin(ang[:, None] * freqs)], -1)
    feat = np.pad(feat, ((0, 0), (0, HY_FEAT_PAD - feat.shape[1])))
    tau = np.concatenate([np.arange(seq_len), np.zeros(1, np.int64), np.arange(seq_len - 1, 0, -1)])
    mask = np.ones(2 * seq_len)
    mask[seq_len] = 0.0
    t_mask = np.stack([t[tau], mask], axis=1)
    max_decay = math.log(HY_TARGET) / HY_SHORT_PCT
    min_decay = math.log(HY_TARGET) / HY_LONG_PCT
    deltas = np.abs(np.linspace(min_decay, max_decay, HY_W))
    return (jnp.asarray(feat[tau].T, F32), jnp.asarray(t_mask, F32),
            jnp.asarray(np.tile(deltas, 2)[None, :], F32))


def hyena_filter_spectra(seq_len, tabs, consts, w1, b1, w2, b2, w3):
    feat_t, t_mask, deltas2 = consts
    w1t = jnp.pad(w1, ((0, HY_FEAT_PAD - w1.shape[0]), (0, 0))).T
    h_t = filter_mlp(feat_t, w1t, b1, jnp.transpose(w2, (0, 2, 1)), b2)
    w3r = w3.reshape(w3.shape[0], 2, 2, HY_W)
    w3_halves = jnp.transpose(w3r, (2, 0, 1, 3)).reshape(2, w3.shape[0], 2 * HY_W)
    full, sabs = filter_expand(h_t, t_mask, w3_halves, deltas2)
    inv_l1 = 1.0 / jnp.sum(sabs, axis=0, keepdims=True)
    a = outer_dft(tabs["wf_filt"], full)
    return filter_spectrum(tabs["tf"], a, inv_l1)


def hyena_mixer(proj, tabs, kf, short_w, bias):
    seq = proj.shape[0]
    z, x1, x2 = short_conv(proj, short_w)
    for order, gate in enumerate((x1, x2)):
        a = outer_dft(tabs["wf_data"], z)
        g = spectral_conv(tabs["tf"], tabs["ti"], a, kf, order)
        z = outer_inverse_gate(tabs["wi_re"], tabs["wi_im"], g, gate, z,
                               bias[order].reshape(1, HY_W).astype(F32), token_major_out=(order == 1))
    return z.reshape(seq, HY_W)


def _chunk_rows(c, t):
    return pl.ds(pl.multiple_of(c * t, t), t)


def _retention_kernel(lg_ref, q_ref, k_ref, v_ref, g_ref, o_ref, rstore_ref, s_ref, *, nblk, cpb, t):
    h = pl.program_id(0)
    sweep = pl.program_id(1)
    i = pl.program_id(2)
    lgf = lg_ref[0, h]
    lgb = lg_ref[1, h]
    pos = lax.broadcasted_iota(jnp.int32, (t, 1), 0).astype(F32)
    chunk_len = jnp.full((1, RET_DV), float(t), F32)

    @pl.when(i == 0)
    def _():
        s_ref[...] = jnp.zeros_like(s_ref)

    @pl.when(sweep == 0)
    def _():
        def body(c, carry):
            cc = cpb - 1 - c
            rows = _chunk_rows(cc, t)
            rstore_ref[(nblk - 1 - i) * cpb + cc] = s_ref[...]
            kw = (k_ref[rows, :].astype(F32) * jnp.exp(lgb * pos)).astype(BF16)
            s_ref[...] = s_ref[...] * jnp.exp(lgb * chunk_len) + _dot_tn(kw, v_ref[rows, :])
            return carry

        lax.fori_loop(0, cpb, body, 0)

    @pl.when(sweep == 1)
    def _():
        ri = lax.broadcasted_iota(jnp.int32, (t, t), 0)
        ci = lax.broadcasted_iota(jnp.int32, (t, t), 1)
        diff = (ri - ci).astype(F32)
        decay = (jnp.where(diff >= 0, jnp.exp(lgf * jnp.maximum(diff, 0.0)), 0.0)
                 + jnp.where(diff <= 0, jnp.exp(lgb * jnp.maximum(-diff, 0.0)), 0.0))
        q_fwd = jnp.exp(lgf * (pos + 1.0)) * (RET_DK ** -0.5)
        q_bwd = jnp.exp(lgb * (t - pos)) * (RET_DK ** -0.5)
        k_fwd = jnp.exp(lgf * (t - 1.0 - pos))

        def body(c, carry):
            rows = _chunk_rows(c, t)
            q = q_ref[rows, :].astype(F32)
            k = k_ref[rows, :]
            v = v_ref[rows, :]
            scores = _dot_nt((q * (RET_DK ** -0.5)).astype(BF16), k) * decay
            y = _dot(scores.astype(BF16), v)
            y = y + _dot((q * q_fwd).astype(BF16), s_ref[...].astype(BF16))
            y = y + _dot((q * q_bwd).astype(BF16), rstore_ref[i * cpb + c].astype(BF16))
            kw = (k.astype(F32) * k_fwd).astype(BF16)
            s_ref[...] = s_ref[...] * jnp.exp(lgf * chunk_len) + _dot_tn(kw, v)
            y = y * lax.rsqrt(jnp.mean(y * y, axis=-1, keepdims=True) + EPS)
            gt = g_ref[rows, :].astype(F32)
            o_ref[rows, :] = (gt * jax.nn.sigmoid(gt) * y).astype(o_ref.dtype)
            return carry

        lax.fori_loop(0, cpb, body, 0)


CHUNKS_PER_STEP = 4


def retention_mixer(proj, col0, log_decay, t=CHUNK):
    seq = proj.shape[0]
    nc = seq // t
    cpb = math.gcd(CHUNKS_PER_STEP, nc)
    nblk = nc // cpb
    tb = cpb * t
    qb = col0 // RET_DK
    kb = qb + RET_H
    vb = (col0 + 2 * RET_H * RET_DK) // RET_DV
    gb = vb + RET_H

    def rows(sweep, i):
        return sweep * i + (1 - sweep) * (nblk - 1 - i)

    return pl.pallas_call(
        functools.partial(_retention_kernel, nblk=nblk, cpb=cpb, t=t),
        grid=(RET_H, 2, nblk),
        in_specs=[pl.BlockSpec(memory_space=pltpu.SMEM),
                  pl.BlockSpec((tb, RET_DK), lambda h, s, i: (s * i, qb + h)),
                  pl.BlockSpec((tb, RET_DK), lambda h, s, i: (rows(s, i), kb + h)),
                  pl.BlockSpec((tb, RET_DV), lambda h, s, i: (rows(s, i), vb + h)),
                  pl.BlockSpec((tb, RET_DV), lambda h, s, i: (s * i, gb + h))],
        out_specs=pl.BlockSpec((tb, RET_DV), lambda h, s, i: (s * i, h)),
        out_shape=jax.ShapeDtypeStruct((seq, RET_H * RET_DV), BF16),
        scratch_shapes=[pltpu.VMEM((nc, RET_DK, RET_DV), F32), pltpu.VMEM((RET_DK, RET_DV), F32)],
        compiler_params=_cparams("arbitrary", "arbitrary", "arbitrary"),
        name="retention",
    )(log_decay, proj, proj, proj, proj)


def _log_sigmoid(x):
    return jnp.minimum(x, 0.0) - jnp.log(1.0 + jnp.exp(-jnp.abs(x)))


def _mlstm_gates(gc_ref, gr_ref, bias_ref, h, direction, rows):
    bi = bias_ref[direction * 2 * ML_H + h]
    bf = bias_ref[direction * 2 * ML_H + ML_H + h]
    a = 2 * direction
    ig_c = gc_ref[0, rows, a:a + 1] + bi
    lf_c = _log_sigmoid(gc_ref[0, rows, a + 1:a + 2] + bf)
    ig_r = gr_ref[0, a:a + 1, rows] + bi
    lf_r = _log_sigmoid(gr_ref[0, a + 1:a + 2, rows] + bf)
    return ig_c, lf_c, ig_r, lf_r


def _split3(x):
    hi = x.astype(BF16)
    rest = x - hi.astype(F32)
    mid = rest.astype(BF16)
    return hi, mid, (rest - mid.astype(F32)).astype(BF16)


def _running_sums(lf_fwd, lf_bwd, as_rows):
    t = lf_fwd.shape[1] if as_rows else lf_fwd.shape[0]
    ri = lax.broadcasted_iota(jnp.int32, (t, t), 0)
    ci = lax.broadcasted_iota(jnp.int32, (t, t), 1)
    tri = (ri >= ci).astype(BF16)
    if as_rows:
        sel = lax.broadcasted_iota(jnp.int32, (8, t), 0)
        both = jnp.where(sel == 0, lf_fwd, jnp.where(sel == 1, lf_bwd, 0.0))
        left = sum(_dot_nt(p, tri) for p in _split3(both))
        left_f, left_b = left[0:1, :], left[1:2, :]
        total_b = jnp.sum(lf_bwd, axis=1, keepdims=True)
    else:
        sel = lax.broadcasted_iota(jnp.int32, (t, LANES), 1)
        both = jnp.where(sel == 0, lf_fwd, jnp.where(sel == 1, lf_bwd, 0.0))
        left = sum(_dot(tri, p) for p in _split3(both))
        left_f, left_b = left[:, 0:1], left[:, 1:2]
        total_b = jnp.sum(lf_bwd, axis=0, keepdims=True)
    return left_f, total_b - left_b + lf_bwd


def _mlstm_state_step(k, v, ig_c, cum_c, total, c_ref, n_ref, m_ref):
    a = total - cum_c + ig_c
    m_loc = jnp.max(a, axis=0, keepdims=True)
    kw = k * jnp.exp(a - m_loc)
    kv = _dot_tn(kw.astype(BF16), v)
    ksum = jnp.sum(kw, axis=0, keepdims=True)
    m_old = m_ref[0:1, 0:1]
    m_new = jnp.maximum(total + m_old, m_loc)
    sp = jnp.exp(total + m_old - m_new)
    sc = jnp.exp(m_loc - m_new)
    c_ref[...] = sp * c_ref[...] + sc * kv
    n_ref[...] = sp * n_ref[...] + sc * jnp.broadcast_to(ksum, n_ref.shape)
    m_ref[...] = jnp.broadcast_to(m_new, m_ref.shape)


def _mlstm_output(qk, q, v, ig_r, cum_c, cum_r, c_prev, n_prev, m_prev, backward):
    t = q.shape[0]
    ri = lax.broadcasted_iota(jnp.int32, (t, t), 0)
    ci = lax.broadcasted_iota(jnp.int32, (t, t), 1)
    keep = (ri <= ci) if backward else (ri >= ci)
    dlog = jnp.where(keep, cum_c - cum_r + ig_r, -jnp.inf)
    inter = cum_c + m_prev
    m_t = jnp.maximum(inter, jnp.max(dlog, axis=-1, keepdims=True))
    s = qk * jnp.exp(dlog - m_t)
    wi = jnp.exp(inter - m_t)
    num = _dot(s.astype(BF16), v) + wi * _dot(q.astype(BF16), c_prev.astype(BF16))
    den = jnp.sum(s, axis=-1, keepdims=True) + wi * jnp.sum(q * n_prev, axis=-1, keepdims=True)
    return num / jnp.maximum(jnp.abs(den), jnp.exp(-m_t))


def _mlstm_kernel(bias_ref, q_ref, k_ref, v_ref, o_ref, gc_ref, gr_ref, gain_ref, out_ref,
                  cstore_ref, nstore_ref, mstore_ref, c_ref, n_ref, m_ref, *, nblk, cpb, t):
    h = pl.program_id(0)
    sweep = pl.program_id(1)
    i = pl.program_id(2)

    @pl.when(i == 0)
    def _():
        c_ref[...] = jnp.zeros_like(c_ref)
        n_ref[...] = jnp.zeros_like(n_ref)
        m_ref[...] = jnp.zeros_like(m_ref)

    @pl.when(sweep == 0)
    def _():
        def body(c, carry):
            cc = cpb - 1 - c
            rows = _chunk_rows(cc, t)
            n = (nblk - 1 - i) * cpb + cc
            cstore_ref[n] = c_ref[...]
            nstore_ref[n] = n_ref[...]
            mstore_ref[n] = m_ref[...]
            k = k_ref[rows, :].astype(F32) * (ML_DK ** -0.5)
            _, lf_c, _, _ = _mlstm_gates(gc_ref, gr_ref, bias_ref, h, 0, rows)
            ig_c, lb_c, _, _ = _mlstm_gates(gc_ref, gr_ref, bias_ref, h, 1, rows)
            _, cumb_c = _running_sums(lf_c, lb_c, False)
            total = jnp.sum(lb_c, axis=0, keepdims=True)
            _mlstm_state_step(k, v_ref[rows, :], ig_c, cumb_c, total, c_ref, n_ref, m_ref)
            return carry

        lax.fori_loop(0, cpb, body, 0)

    @pl.when(sweep == 1)
    def _():
        def body(c, carry):
            rows = _chunk_rows(c, t)
            n = i * cpb + c
            k = k_ref[rows, :].astype(F32) * (ML_DK ** -0.5)
            v = v_ref[rows, :]
            q = q_ref[rows, :].astype(F32)
            qk = _dot_nt(q_ref[rows, :], k.astype(BF16))
            ig_c, lf_c, ig_r, lf_r = _mlstm_gates(gc_ref, gr_ref, bias_ref, h, 0, rows)
            _, lb_c, igb_r, lb_r = _mlstm_gates(gc_ref, gr_ref, bias_ref, h, 1, rows)
            cum_c, cumb_c = _running_sums(lf_c, lb_c, False)
            cum_r, cumb_r = _running_sums(lf_r, lb_r, True)
            hf = _mlstm_output(qk, q, v, ig_r, cum_c, cum_r, c_ref[...], n_ref[0:1, :], m_ref[0:1, 0:1], False)
            total = jnp.sum(lf_c, axis=0, keepdims=True)
            _mlstm_state_step(k, v, ig_c, cum_c, total, c_ref, n_ref, m_ref)
            hb = _mlstm_output(qk, q, v, igb_r, cumb_c, cumb_r, cstore_ref[n], nstore_ref[n][0:1, :],
                               mstore_ref[n][0:1, 0:1], True)
            y = hf + hb
            y = y * lax.rsqrt(jnp.mean(y * y, axis=-1, keepdims=True) + EPS) * gain_ref[...]
            out_ref[rows, :] = (jax.nn.sigmoid(o_ref[rows, :].astype(F32)) * y).astype(out_ref.dtype)
            return carry

        lax.fori_loop(0, cpb, body, 0)


def mlstm_mixer(proj, gates, col0, gate_bias, norm_gain, t=CHUNK):
    seq = proj.shape[0]
    nc = seq // t
    cpb = math.gcd(CHUNKS_PER_STEP, nc)
    nblk = nc // cpb
    tb = cpb * t
    qb = col0 // ML_DK
    kb = qb + ML_H
    vb = (col0 + 2 * ML_H * ML_DK) // ML_DV
    ob = vb + ML_H
    g = gates[:, :4 * ML_H].reshape(seq, 2, 2, ML_H)
    g = jnp.transpose(g, (3, 0, 1, 2)).reshape(ML_H, seq, 4)
    g_cols = g
    g_rows = jnp.transpose(g, (0, 2, 1))

    def rows(sweep, i):
        return sweep * i + (1 - sweep) * (nblk - 1 - i)

    return pl.pallas_call(
        functools.partial(_mlstm_kernel, nblk=nblk, cpb=cpb, t=t),
        grid=(ML_H, 2, nblk),
        in_specs=[pl.BlockSpec(memory_space=pltpu.SMEM),
                  pl.BlockSpec((tb, ML_DK), lambda h, s, i: (s * i, qb + h)),
                  pl.BlockSpec((tb, ML_DK), lambda h, s, i: (rows(s, i), kb + h)),
                  pl.BlockSpec((tb, ML_DV), lambda h, s, i: (rows(s, i), vb + h)),
                  pl.BlockSpec((tb, ML_DV), lambda h, s, i: (s * i, ob + h)),
                  pl.BlockSpec((1, tb, 4), lambda h, s, i: (h, rows(s, i), 0)),
                  pl.BlockSpec((1, 4, tb), lambda h, s, i: (h, 0, rows(s, i))),
                  pl.BlockSpec((1, ML_DV), lambda h, s, i: (0, h))],
        out_specs=pl.BlockSpec((tb, ML_DV), lambda h, s, i: (s * i, h)),
        out_shape=jax.ShapeDtypeStruct((seq, ML_H * ML_DV), BF16),
        scratch_shapes=[pltpu.VMEM((nc, ML_DK, ML_DV), F32), pltpu.VMEM((nc, 8, ML_DK), F32),
                        pltpu.VMEM((nc, 8, LANES), F32), pltpu.VMEM((ML_DK, ML_DV), F32),
                        pltpu.VMEM((8, ML_DK), F32), pltpu.VMEM((8, LANES), F32)],
        compiler_params=_cparams("arbitrary", "arbitrary", "arbitrary"),
        name="mlstm",
    )(gate_bias, proj, proj, proj, proj, g_cols, g_rows, norm_gain.reshape(1, ML_H * ML_DV))


PERM_ROWS = 256
CLASS_RUN = BF16_TILE_ROWS
ATT_SUB_ROWS = 128


def _group_permutation(dilation):
    run = CLASS_RUN * dilation
    new = jnp.arange(PERM_ROWS, dtype=jnp.int32)
    within = new % run
    src = (new // run) * run + (within % CLASS_RUN) * dilation + within // CLASS_RUN
    return (src[:, None] == jnp.arange(PERM_ROWS, dtype=jnp.int32)[None, :]).astype(BF16)


def _attention_prep_kernel(gain_ref, p4_ref, p16_ref, x_ref, o1_ref, o4_ref, o16_ref):
    part = pl.program_id(1)

    @pl.when(part < 2)
    def _():
        gain = gain_ref[0]
        for h in range(ATT_H):
            cols = slice(h * ATT_DH, (h + 1) * ATT_DH)
            x = x_ref[:, cols].astype(F32)
            o1_ref[:, cols] = (x * lax.rsqrt(jnp.mean(x * x, axis=-1, keepdims=True) + EPS) * gain).astype(BF16)

    @pl.when(part == 2)
    def _():
        o1_ref[...] = x_ref[...]

    for s in range(x_ref.shape[0] // PERM_ROWS):
        rows = slice(s * PERM_ROWS, (s + 1) * PERM_ROWS)
        x = o1_ref[rows, :]
        o4_ref[rows, :] = _dot(p4_ref[...], x).astype(BF16)
        o16_ref[rows, :] = _dot(p16_ref[...], x).astype(BF16)


def attention_prep(proj, col0, qk_gain, tr=512):
    seq = proj.shape[0]
    cb = col0 // W_GROUP
    gains = jnp.stack([qk_gain[0] * (ATT_DH ** -0.5), qk_gain[1], jnp.ones_like(qk_gain[0])]).reshape(3, 1, ATT_DH)
    out = jax.ShapeDtypeStruct((seq, 3 * W_GROUP), BF16)
    ospec = pl.BlockSpec((tr, W_GROUP), lambda i, j: (i, j))
    pspec = pl.BlockSpec((PERM_ROWS, PERM_ROWS), lambda i, j: (0, 0))
    return pl.pallas_call(
        _attention_prep_kernel,
        grid=(seq // tr, 3),
        in_specs=[pl.BlockSpec((1, 1, ATT_DH), lambda i, j: (j, 0, 0)), pspec, pspec,
                  pl.BlockSpec((tr, W_GROUP), lambda i, j: (i, cb + j))],
        out_specs=[ospec, ospec, ospec],
        out_shape=[out, out, out],
        compiler_params=_cparams("parallel", "arbitrary"),
        name="attention_prep",
    )(gains.astype(F32), _group_permutation(4), _group_permutation(16), proj)


def _band_attention_kernel(q_ref, kp_ref, kc_ref, kn_ref, vp_ref, vc_ref, vn_ref, o_ref, lse_ref,
                           *, dilation, nblk):
    i = pl.program_id(1)
    tq = q_ref.shape[0] * CLASS_RUN
    hs = ATT_HALF_STEPS
    sq = min(ATT_SUB_ROWS, tq)
    sk = sq + 2 * hs
    ri = lax.broadcasted_iota(jnp.int32, (sq, sk), 0)
    ci = lax.broadcasted_iota(jnp.int32, (sq, sk), 1)
    off = ci - hs - ri
    first_col = jnp.where(i > 0, 0, hs)
    end_col = jnp.where(i < nblk - 1, tq + 2 * hs, tq + hs)
    in_band = jnp.abs(off) <= hs
    dist = (jnp.abs(off) * dilation).astype(F32)
    valid = [in_band & (ci + s0 >= first_col) & (ci + s0 < end_col) for s0 in range(0, tq, sq)]
    lane = lax.broadcasted_iota(jnp.int32, (sq, LANES), 1)
    lse_all = [jnp.zeros((sq, LANES), F32) for _ in valid]

    def rows(ref, cols):
        x = ref[:, :, cols]
        return x.reshape(x.shape[0] * CLASS_RUN, x.shape[2])

    for h in range(ATT_H):
        cols = slice(h * ATT_DH, (h + 1) * ATT_DH)
        slope = 2.0 ** (-8.0 * (h + 1) / ATT_H)
        qq = rows(q_ref, cols)
        kk = jnp.concatenate([rows(kp_ref, cols), rows(kc_ref, cols), rows(kn_ref, cols)], axis=0)
        vv = jnp.concatenate([rows(vp_ref, cols), rows(vc_ref, cols), rows(vn_ref, cols)], axis=0)
        outs = []
        for b, s0 in enumerate(range(0, tq, sq)):
            s = _dot_nt(qq[s0:s0 + sq], kk[s0:s0 + sk]) - slope * dist
            s = jnp.where(valid[b], s, NEG)
            m = jnp.max(s, axis=-1, keepdims=True)
            p = jnp.exp(s - m)
            den = jnp.sum(p, axis=-1, keepdims=True)
            outs.append(_dot(p.astype(BF16), vv[s0:s0 + sk]) / den)
            lse_all[b] = jnp.where(lane == h, m + jnp.log(den), lse_all[b])
        o = jnp.concatenate(outs, axis=0)
        o_ref[:, :, cols] = o.astype(o_ref.dtype).reshape(tq // CLASS_RUN, CLASS_RUN, ATT_DH)
    lse_ref[...] = jnp.concatenate(lse_all, axis=0).reshape(tq // CLASS_RUN, CLASS_RUN, LANES)


def band_attention(qkv, dilation, tq=256):
    seq = qkv.shape[0]
    n = seq // dilation
    tq = min(tq, n)
    nblk = n // tq
    hs = ATT_HALF_STEPS
    runs = n // CLASS_RUN
    tr = tq // CLASS_RUN
    hr = hs // CLASS_RUN
    ratio = tq // hs
    last_halo = n // hs - 1
    view = qkv.reshape(runs, dilation, CLASS_RUN, 3 * W_GROUP)

    def cur(part):
        return pl.BlockSpec((tr, None, CLASS_RUN, W_GROUP), lambda r, i: (i, r, 0, part))

    def prev(part):
        return pl.BlockSpec((hr, None, CLASS_RUN, W_GROUP), lambda r, i: (jnp.maximum(i * ratio - 1, 0), r, 0, part))

    def nxt(part):
        return pl.BlockSpec((hr, None, CLASS_RUN, W_GROUP),
                            lambda r, i: (jnp.minimum((i + 1) * ratio, last_halo), r, 0, part))

    o, lse = pl.pallas_call(
        functools.partial(_band_attention_kernel, dilation=dilation, nblk=nblk),
        grid=(dilation, nblk),
        in_specs=[cur(0), prev(1), cur(1), nxt(1), prev(2), cur(2), nxt(2)],
        out_specs=[pl.BlockSpec((tr, None, CLASS_RUN, W_GROUP), lambda r, i: (i, r, 0, 0)),
                   pl.BlockSpec((tr, None, CLASS_RUN, LANES), lambda r, i: (i, r, 0, 0))],
        out_shape=[jax.ShapeDtypeStruct((runs, dilation, CLASS_RUN, W_GROUP), BF16),
                   jax.ShapeDtypeStruct((runs, dilation, CLASS_RUN, LANES), F32)],
        compiler_params=_cparams("parallel", "arbitrary"),
        name=f"band_attention_d{dilation}",
    )(view, view, view, view, view, view, view)
    return o.reshape(seq, W_GROUP), lse.reshape(seq, LANES)


def _merge_branches_kernel(q4_ref, q16_ref, o1_ref, o4_ref, o16_ref, l1_ref, l4_ref, l16_ref, out_ref):
    def ungroup(qt, x):
        return _dot(qt, x)

    def ungroup_f32(qt, x):
        hi = x.astype(BF16)
        lo = (x - hi.astype(F32)).astype(BF16)
        return _dot(qt, hi) + _dot(qt, lo)

    for s in range(o1_ref.shape[0] // PERM_ROWS):
        rows = slice(s * PERM_ROWS, (s + 1) * PERM_ROWS)
        q4, q16 = q4_ref[...], q16_ref[...]
        l1 = l1_ref[rows, :]
        l2 = ungroup_f32(q4, l4_ref[rows, :])
        l3 = ungroup_f32(q16, l16_ref[rows, :])
        m = jnp.maximum(jnp.maximum(l1, l2), l3)
        e1, e2, e3 = jnp.exp(l1 - m), jnp.exp(l2 - m), jnp.exp(l3 - m)
        inv = 1.0 / (e1 + e2 + e3)
        w1, w2, w3 = e1 * inv, e2 * inv, e3 * inv
        o2 = ungroup(q4, o4_ref[rows, :])
        o3 = ungroup(q16, o16_ref[rows, :])
        for h in range(ATT_H):
            cols = slice(h * ATT_DH, (h + 1) * ATT_DH)
            out_ref[rows, cols] = (w1[:, h:h + 1] * o1_ref[rows, cols].astype(F32)
                                   + w2[:, h:h + 1] * o2[:, cols]
                                   + w3[:, h:h + 1] * o3[:, cols]).astype(out_ref.dtype)


def merge_branches(outs, lses, tr=512):
    seq = outs[0].shape[0]
    ospec = pl.BlockSpec((tr, W_GROUP), lambda i: (i, 0))
    lspec = pl.BlockSpec((tr, LANES), lambda i: (i, 0))
    pspec = pl.BlockSpec((PERM_ROWS, PERM_ROWS), lambda i: (0, 0))
    return pl.pallas_call(
        _merge_branches_kernel,
        grid=(seq // tr,),
        in_specs=[pspec, pspec, ospec, ospec, ospec, lspec, lspec, lspec],
        out_specs=ospec,
        out_shape=jax.ShapeDtypeStruct((seq, W_GROUP), BF16),
        compiler_params=_cparams("parallel"),
        name="attention_merge",
    )(_group_permutation(4).T, _group_permutation(16).T, *outs, *lses)


def dilated_attention(proj, col0, qk_gain):
    grouped = attention_prep(proj, col0, qk_gain)
    outs, lses = [], []
    for d, qkv in zip(ATT_DILATIONS, grouped):
        o, lse = band_attention(qkv, d)
        outs.append(o)
        lses.append(lse)
    return merge_branches(outs, lses)


def kernel(x, c, ada_w, ada_b, ada_table, w_in, w_out, hy_short, hy_w1, hy_b1, hy_w2, hy_b2, hy_w3, hy_bias,
           ret_decay, att_qk_gain, ml_gate_bias, ml_norm_gain, ffn_w1, ffn_w3, ffn_w2):
    batch, seq, d_model = x.shape
    depth = w_in.shape[0]
    d_main = 12 * W_GROUP
    hidden = ffn_w1.shape[2]

    tabs = _dft_tables(seq)
    consts = _filter_positions(seq)
    mod_shared = ada_modulation(c, ada_w, ada_b)

    w_gate_b = jnp.pad(w_in[:, :, d_main:], ((0, 0), (0, 0), (0, LANES - (w_in.shape[2] - d_main)))).astype(BF16)
    first = (w_in[0, :, :d_main].astype(BF16), w_out[0].astype(BF16),
             ffn_w1[0].astype(BF16), ffn_w3[0].astype(BF16), ffn_w2[0].astype(BF16))

    rows = x.reshape(batch * seq, d_model)
    outs = []
    for b in range(batch):
        xb = rows[b * seq:(b + 1) * seq]
        w_in_b, w_out_b, w1_b, w3_b, w2_b = first
        for l in range(depth):
            mod = (mod_shared[b:b + 1] + ada_table[l].reshape(1, -1)).reshape(6, d_model)
            sh1, sc1, g1, sh2, sc2, g2 = (mod[i:i + 1] for i in range(6))
            proj, gates = norm_proj(xb, 1.0 + sc1, sh1, w_in_b[None], 0, d_main, w_gate_b[l])

            kf = hyena_filter_spectra(seq, tabs, consts, hy_w1[l], hy_b1[l], hy_w2[l], hy_b2[l], hy_w3[l])
            y_a = hyena_mixer(proj, tabs, kf, hy_short[l], hy_bias[l])
            y_b = retention_mixer(proj, 3 * W_GROUP, jax.nn.log_sigmoid(ret_decay[l].astype(F32)))
            y_c = dilated_attention(proj, 6 * W_GROUP, att_qk_gain[l])
            y_d = mlstm_mixer(proj, gates, 9 * W_GROUP, ml_gate_bias[l], ml_norm_gain[l])
            y = jnp.concatenate([y_a.astype(BF16), y_b, y_c, y_d], axis=-1)
            xb, _ = mm_residual(y, w_out_b[None], 0, xb, g1, tm=1024, tn=1024, tk=y.shape[1])

            nxt = l + 1
            more = nxt < depth
            ffn_cast = ((ffn_w1, nxt, False), (ffn_w3, nxt, False), (ffn_w2, nxt, True)) if more else ()
            u, ffn_next = norm_swiglu(xb, 1.0 + sc2, sh2, w1_b, w3_b, ffn_cast)
            mix_cast = ((w_in, nxt, d_main), (w_out, nxt, d_model)) if more else ()
            xb, mix_next = mm_residual(u, w2_b[None], 0, xb, g2, tm=1024, tn=512, tk=hidden // 2, to_cast=mix_cast)
            if more:
                (w1_b, w3_b, w2_b), (w_in_b, w_out_b) = ffn_next, mix_next
        outs.append(xb)
    return jnp.concatenate(outs, 0).reshape(batch, seq, d_model)
```
